```python
import math
import jax, jax.numpy as jnp
from jax import lax
import numpy as np

D_MODEL = 1024
BATCH = 8
SEQ = 2048
DEPTH = 1

ATTN_HEADS = 16
ATTN_KV_HEADS = 4
ATTN_HEAD_DIM = 64
ATTN_Q_WIDTH = ATTN_HEADS * ATTN_HEAD_DIM
ATTN_KV_WIDTH = ATTN_KV_HEADS * ATTN_HEAD_DIM
WINDOW = 128
ATTN_BLOCK = 128
DN_HEADS = 8
DN_KEY_DIM = 128
DN_VAL_DIM = 128
DN_QK_WIDTH = DN_HEADS * DN_KEY_DIM
DN_V_WIDTH = DN_HEADS * DN_VAL_DIM
DN_CONV = 4
DN_CONV_CH = 2 * DN_QK_WIDTH + DN_V_WIDTH
DN_CHUNK = 64
PEER_HEADS = 8
PEER_N_KEYS = 128
PEER_N_EXPERTS = PEER_N_KEYS * PEER_N_KEYS
PEER_HALF = 128
PEER_QUERY_DIM = 2 * PEER_HALF
PEER_TOPK = 16
PEER_TOKEN_BLOCK = 128
IN_SIZES = (ATTN_Q_WIDTH, ATTN_KV_WIDTH, ATTN_KV_WIDTH,
            DN_QK_WIDTH, DN_QK_WIDTH, DN_V_WIDTH, DN_V_WIDTH, DN_HEADS, DN_HEADS,
            D_MODEL, D_MODEL)
IN_COLS = sum(IN_SIZES)
EPS = 1e-6
NEG = -1e30

kernel_name = 'hybrid_swa_gdn_peer_block'


def rms_norm(x, gain):
    xf = x.astype(jnp.float32)
    y = xf * lax.rsqrt(jnp.mean(xf * xf, axis=-1, keepdims=True) + EPS)
    return (y * gain.astype(jnp.float32)).astype(x.dtype)


def l2_norm(x):
    xf = x.astype(jnp.float32)
    return xf * lax.rsqrt(jnp.sum(xf * xf, axis=-1, keepdims=True) + EPS)


def split_points():
    pts, acc = [], 0
    for s in IN_SIZES[:-1]:
        acc += s
        pts.append(acc)
    return pts


def causal_depthwise_conv(x, w):
    k_len, ch = w.shape
    return lax.conv_general_dilated(
        x, w[:, None, :].astype(x.dtype), window_strides=(1,), padding=[(k_len - 1, 0)],
        dimension_numbers=('NWC', 'WIO', 'NWC'), feature_group_count=ch)


def sliding_window_attention(q, k, v, sinks):
    b, s = q.shape[:2]
    nb = s // ATTN_BLOCK
    grp = ATTN_HEADS // ATTN_KV_HEADS
    qb = q.reshape(b, nb, ATTN_BLOCK, ATTN_KV_HEADS, grp, ATTN_HEAD_DIM)

    def with_prev(t):
        t = t.reshape(b, nb, ATTN_BLOCK, ATTN_KV_HEADS, ATTN_HEAD_DIM)
        prev = jnp.pad(t[:, :-1], ((0, 0), (1, 0), (0, 0), (0, 0), (0, 0)))
        return jnp.concatenate([prev, t], axis=2)

    kk, vv = with_prev(k), with_prev(v)
    scores = jnp.einsum('bnqhgd,bnkhd->bnhgqk', qb, kk).astype(jnp.float32) * (ATTN_HEAD_DIM ** -0.5)
    qi = jnp.arange(ATTN_BLOCK)[:, None]
    kj = jnp.arange(2 * ATTN_BLOCK)[None, :]
    rel = qi + ATTN_BLOCK - kj
    band = (rel >= 0) & (rel < WINDOW)
    blk = jnp.arange(nb)[:, None, None]
    mask = band[None] & ((blk > 0) | (kj >= ATTN_BLOCK)[None])
    scores = jnp.where(mask[None, :, None, None], scores, NEG)
    sink = jnp.broadcast_to(sinks.astype(jnp.float32).reshape(1, 1, ATTN_KV_HEADS, grp, 1, 1),
                            scores.shape[:-1] + (1,))
    probs = jax.nn.softmax(jnp.concatenate([scores, sink], axis=-1), axis=-1)[..., :-1]
    out = jnp.einsum('bnhgqk,bnkhd->bnqhgd', probs.astype(v.dtype), vv)
    return out.reshape(b, s, ATTN_Q_WIDTH)


def gated_delta_rule(q, k, v, g, beta):
    b, s, h, dk = q.shape
    dv = v.shape[-1]
    nc, cl = s // DN_CHUNK, DN_CHUNK

    def chunks(t):
        return t.astype(jnp.float32).reshape(b, nc, cl, h, t.shape[-1]).transpose(0, 3, 1, 2, 4)

    qc = chunks(q) * (dk ** -0.5)
    kc, vc = chunks(k), chunks(v)
    betac = beta.astype(jnp.float32).reshape(b, nc, cl, h).transpose(0, 3, 1, 2)
    gc = jnp.cumsum(g.astype(jnp.float32).reshape(b, nc, cl, h).transpose(0, 3, 1, 2), axis=-1)
    k_beta = kc * betac[..., None]
    v_beta = vc * betac[..., None]
    tril = jnp.tril(jnp.ones((cl, cl), dtype=bool))
    strict = jnp.tril(jnp.ones((cl, cl), dtype=bool), -1)
    diff = gc[..., :, None] - gc[..., None, :]
    decay = jnp.where(tril, jnp.exp(jnp.where(tril, diff, 0.0)), 0.0)
    a_low = jnp.where(strict, jnp.einsum('bhncd,bhnsd->bhncs', k_beta, kc) * decay, 0.0)
    eye = jnp.eye(cl, dtype=jnp.float32)
    t_mat = lax.linalg.triangular_solve(eye + a_low, jnp.broadcast_to(eye, a_low.shape),
                                        left_side=True, lower=True)
    u = jnp.einsum('bhncs,bhnse->bhnce', t_mat, v_beta)
    w = jnp.einsum('bhncs,bhnsd->bhncd', t_mat, k_beta * jnp.exp(gc)[..., None])
    qk = jnp.einsum('bhncd,bhnsd->bhncs', qc, kc) * decay

    def step(state, inp):
        q_i, k_i, u_i, w_i, g_i, qk_i = inp
        v_new = u_i - jnp.einsum('bhcd,bhde->bhce', w_i, state)
        o = (jnp.einsum('bhcd,bhde->bhce', q_i * jnp.exp(g_i)[..., None], state)
             + jnp.einsum('bhcs,bhse->bhce', qk_i, v_new))
        g_last = g_i[..., -1]
        state = (state * jnp.exp(g_last)[..., None, None]
                 + jnp.einsum('bhcd,bhce->bhde', k_i * jnp.exp(g_last[..., None] - g_i)[..., None], v_new))
        return state, o

    xs = tuple(jnp.moveaxis(t, 2, 0) for t in (qc, kc, u, w, gc, qk))
    state0 = jnp.zeros((b, h, dk, dv), jnp.float32)
    _, o = lax.scan(step, state0, xs)
    return o.transpose(1, 0, 3, 2, 4).reshape(b, s, h, dv)


def peer(xn, w_q, sub_keys, u_tab, v_tab):
    b, s, d = xn.shape
    q = (xn @ w_q).reshape(b, s, PEER_HEADS, 2, PEER_HALF)
    scores = jnp.einsum('bshpd,hpkd->bshpk', q, sub_keys).astype(jnp.float32)
    top_s, top_i = lax.top_k(scores, PEER_TOPK)
    cand_s = (top_s[..., 0, :, None] + top_s[..., 1, None, :]).reshape(b, s, PEER_HEADS, PEER_TOPK * PEER_TOPK)
    cand_i = (top_i[..., 0, :, None] * PEER_N_KEYS + top_i[..., 1, None, :]).reshape(b, s, PEER_HEADS, PEER_TOPK * PEER_TOPK)
    best_s, best_pos = lax.top_k(cand_s, PEER_TOPK)
    idx = jnp.take_along_axis(cand_i, best_pos, axis=-1)
    gate = jax.nn.softmax(best_s, axis=-1)
    n_tok = b * s
    nblk = n_tok // PEER_TOKEN_BLOCK
    sel = PEER_HEADS * PEER_TOPK
    xt = xn.reshape(nblk, PEER_TOKEN_BLOCK, d)
    it = idx.reshape(nblk, PEER_TOKEN_BLOCK, sel)
    gt = gate.reshape(nblk, PEER_TOKEN_BLOCK, sel).astype(xn.dtype)

    def block(args):
        xb, ib, gb = args
        ub = jnp.take(u_tab, ib, axis=0)
        vb = jnp.take(v_tab, ib, axis=0)
        hid = jax.nn.gelu(jnp.einsum('td,ted->te', xb, ub), approximate=False)
        return jnp.einsum('te,ted->td', hid * gb, vb)

    return lax.map(block, (xt, it, gt)).reshape(b, s, d)


def setup_inputs(seed: int = 0) -> dict:
    key = jax.random.key(seed)
    ks = jax.random.split(key, 22)
    L, D = DEPTH, D_MODEL

    def nrm(k, shape, scale):
        return jax.random.normal(k, shape, jnp.float32) * scale

    dt = jnp.exp(jax.random.uniform(ks[11], (L, DN_HEADS), jnp.float32,
                                    minval=math.log(1e-3), maxval=math.log(1e-1)))
    return {
        'x': nrm(ks[0], (BATCH, SEQ, D), 1.0),
        'c': nrm(ks[1], (BATCH, D), 1.0),
        'w_ada': nrm(ks[2], (L, D, 6 * D), 0.5 * D ** -0.5),
        'b_ada': nrm(ks[3], (L, 6 * D), 0.02),
        'norm1_gain': 1.0 + nrm(ks[4], (L, D), 0.02),
        'w_in': nrm(ks[5], (L, D, IN_COLS), D ** -0.5),
        'attn_q_norm': 1.0 + nrm(ks[6], (L, ATTN_HEAD_DIM), 0.02),
        'attn_k_norm': 1.0 + nrm(ks[7], (L, ATTN_HEAD_DIM), 0.02),
        'attn_sinks': nrm(ks[8], (L, ATTN_HEADS), 1.0),
        'dn_conv_w': nrm(ks[9], (L, DN_CONV, DN_CONV_CH), DN_CONV ** -0.5),
        'dn_a_log': jnp.log(jax.random.uniform(ks[10], (L, DN_HEADS), jnp.float32, minval=1.0, maxval=16.0)),
        'dn_dt_bias': dt + jnp.log(-jnp.expm1(-dt)),
        'dn_out_norm': 1.0 + nrm(ks[12], (L, DN_VAL_DIM), 0.02),
        'w_attn_branch': nrm(ks[13], (L, ATTN_Q_WIDTH, D), ATTN_Q_WIDTH ** -0.5),
        'w_dn_branch': nrm(ks[14], (L, DN_V_WIDTH, D), DN_V_WIDTH ** -0.5),
        'w_out': nrm(ks[15], (L, D, D), D ** -0.5),
        'norm2_gain': 1.0 + nrm(ks[16], (L, D), 0.02),
        'peer_w_q': nrm(ks[17], (L, D, PEER_HEADS * PEER_QUERY_DIM), D ** -0.5),
        'peer_sub_keys': nrm(ks[18], (L, PEER_HEADS, 2, PEER_N_KEYS, PEER_HALF), PEER_HALF ** -0.5),
        'peer_u': nrm(ks[19], (L, PEER_N_EXPERTS, D), D ** -0.5),
        'peer_v': nrm(ks[20], (L, PEER_N_EXPERTS, D), PEER_HEADS ** -0.5),
    }


def reference(x, c, w_ada, b_ada, norm1_gain, w_in, attn_q_norm, attn_k_norm, attn_sinks,
              dn_conv_w, dn_a_log, dn_dt_bias, dn_out_norm, w_attn_branch, w_dn_branch, w_out,
              norm2_gain, peer_w_q, peer_sub_keys, peer_u, peer_v):
    b, s, _ = x.shape
    for l in range(DEPTH):
        mod = jax.nn.silu(c) @ w_ada[l] + b_ada[l]
        shift1, scale1, gate1, shift2, scale2, gate2 = [m[:, None, :] for m in jnp.split(mod, 6, axis=-1)]

        h = rms_norm(x, norm1_gain[l]) * (1.0 + scale1) + shift1
        proj = h @ w_in[l]
        aq, ak, av, dq, dk, dv, dz, dbeta, dalpha, gate_a, gate_b = jnp.split(proj, split_points(), axis=-1)

        qa = rms_norm(aq.reshape(b, s, ATTN_HEADS, ATTN_HEAD_DIM), attn_q_norm[l])
        ka = rms_norm(ak.reshape(b, s, ATTN_KV_HEADS, ATTN_HEAD_DIM), attn_k_norm[l])
        va = av.reshape(b, s, ATTN_KV_HEADS, ATTN_HEAD_DIM)
        y_a = sliding_window_attention(qa, ka, va, attn_sinks[l]) @ w_attn_branch[l]

        qkv = jax.nn.silu(causal_depthwise_conv(jnp.concatenate([dq, dk, dv], axis=-1), dn_conv_w[l]))
        cq, ck, cv = jnp.split(qkv, [DN_QK_WIDTH, 2 * DN_QK_WIDTH], axis=-1)
        qd = l2_norm(cq.reshape(b, s, DN_HEADS, DN_KEY_DIM))
        kd = l2_norm(ck.reshape(b, s, DN_HEADS, DN_KEY_DIM))
        vd = cv.reshape(b, s, DN_HEADS, DN_VAL_DIM)
        beta = jax.nn.sigmoid(dbeta.astype(jnp.float32))
        g = -jnp.exp(dn_a_log[l].astype(jnp.float32)) * jax.nn.softplus(
            dalpha.astype(jnp.float32) + dn_dt_bias[l].astype(jnp.float32))
        od = gated_delta_rule(qd, kd, vd, g, beta).astype(x.dtype)
        od = rms_norm(od, dn_out_norm[l]) * jax.nn.silu(dz.reshape(b, s, DN_HEADS, DN_VAL_DIM))
        y_b = od.reshape(b, s, DN_V_WIDTH) @ w_dn_branch[l]

        mix = (jax.nn.sigmoid(gate_a) * y_a + jax.nn.sigmoid(gate_b) * y_b) @ w_out[l]
        x = x + gate1 * mix

        h2 = rms_norm(x, norm2_gain[l]) * (1.0 + scale2) + shift2
        x = x + gate2 * peer(h2, peer_w_q[l], peer_sub_keys[l], peer_u[l], peer_v[l])
    return x
```

```python
import functools

import jax
import jax.numpy as jnp
from jax import lax
from jax.experimental import pallas as pl
from jax.experimental.pallas import tpu as pltpu

F32 = jnp.float32
BF16 = jnp.bfloat16

D_MODEL = 1024
ATTN_HEADS = 16
ATTN_KV_HEADS = 4
ATTN_HEAD_DIM = 64
ATTN_GROUP = ATTN_HEADS // ATTN_KV_HEADS
ATTN_Q_WIDTH = ATTN_HEADS * ATTN_HEAD_DIM
ATTN_KV_WIDTH = ATTN_KV_HEADS * ATTN_HEAD_DIM
WINDOW = 128
DN_HEADS = 8
DN_DIM = 128
DN_WIDTH = DN_HEADS * DN_DIM
DN_CONV = 4
DN_CHUNK = 64
PEER_HEADS = 8
PEER_N_KEYS = 128
PEER_HALF = 128
PEER_TOPK = 16
EPS = 1e-6
NEG = -1e30
NEG_INF = float("-inf")

VMEM_LIMIT = 48 * 1024 * 1024


def _params(*sem):
    return pltpu.CompilerParams(dimension_semantics=sem, vmem_limit_bytes=VMEM_LIMIT)


def _mm(a, b):
    return jnp.dot(a, b, preferred_element_type=F32)


def _mm_nt(a, b):
    return lax.dot_general(a, b, (((1,), (1,)), ((), ())), preferred_element_type=F32)


def _bmm(a, b):
    return lax.dot_general(a, b, (((2,), (1,)), ((0,), (0,))), preferred_element_type=F32)


def _bmm_nt(a, b):
    return lax.dot_general(a, b, (((2,), (2,)), ((0,), (0,))), preferred_element_type=F32)


def _split(a):
    hi = a.astype(BF16)
    lo = (a - hi.astype(F32)).astype(BF16)
    return hi, lo


def _three_pass(mm, a, b):
    ah, al = _split(a)
    bh, bl = _split(b)
    return mm(ah, bh) + (mm(al, bh) + mm(ah, bl))


def _silu(v):
    return v * jax.nn.sigmoid(v)


def _gelu(v):
    return 0.5 * v * (1.0 + lax.erf(v * (2.0 ** -0.5)))


def _adaln_kernel(c_ref, w_ref, b_ref, o_ref):
    o_ref[...] = _three_pass(_mm, _silu(c_ref[...]), w_ref[...]) + b_ref[...]


def _adaln(c, w_ada, b_ada):
    b, d = c.shape
    n = w_ada.shape[1] // d
    return pl.pallas_call(
        _adaln_kernel,
        out_shape=jax.ShapeDtypeStruct((b, n * d), F32),
        grid=(n,),
        in_specs=[pl.BlockSpec((b, d), lambda j: (0, 0)),
                  pl.BlockSpec((d, d), lambda j: (0, j)),
                  pl.BlockSpec((1, d), lambda j: (0, j))],
        out_specs=pl.BlockSpec((b, d), lambda j: (0, j)),
        compiler_params=_params("arbitrary"),
        name="adaln",
    )(c, w_ada, b_ada.reshape(1, -1))


PROJ_COLS = 512
COL_AQ, COL_DQ, COL_DK, COL_DV, COL_DZ, COL_GA, COL_GB = range(7)
COL_AK = 7 * D_MODEL // ATTN_KV_WIDTH
COL_AV = COL_AK + 1
PROJ_WIDTH = 7 * D_MODEL + 2 * ATTN_KV_WIDTH


def _proj_kernel(x_ref, mod_ref, g_ref, w_ref, wba_ref, wbat_ref, o_ref, ba_ref, bat_ref, h_scr):
    @pl.when(pl.program_id(1) == 0)
    def _():
        x = x_ref[...]
        y = x * lax.rsqrt(jnp.mean(x * x, axis=-1, keepdims=True) + EPS) * g_ref[...]
        mod = mod_ref[...]
        h = y * (1.0 + mod[1:2]) + mod[0:1]
        h_scr[...] = h.astype(BF16)
        ba_ref[...] = _three_pass(_mm, h, wba_ref[...])
        bat_ref[...] = _three_pass(_mm_nt, wbat_ref[...], h)

    o_ref[...] = _mm(h_scr[...], w_ref[...]).astype(BF16)


def _proj(x2, mod3, gain, w_big, w_ba, w_bat, seq, tb):
    t, d = x2.shape
    per_seq = seq // tb
    nba = w_ba.shape[1]
    return pl.pallas_call(
        _proj_kernel,
        out_shape=(jax.ShapeDtypeStruct((t, PROJ_WIDTH), BF16),
                   jax.ShapeDtypeStruct((t, nba), F32),
                   jax.ShapeDtypeStruct((nba, t), F32)),
        grid=(t // tb, PROJ_WIDTH // PROJ_COLS),
        in_specs=[pl.BlockSpec((tb, d), lambda i, j: (i, 0)),
                  pl.BlockSpec((None, 6, d), lambda i, j: (i // per_seq, 0, 0)),
                  pl.BlockSpec((1, d), lambda i, j: (0, 0)),
                  pl.BlockSpec((d, PROJ_COLS), lambda i, j: (0, j)),
                  pl.BlockSpec((d, nba), lambda i, j: (0, 0)),
                  pl.BlockSpec((nba, d), lambda i, j: (0, 0))],
        out_specs=(pl.BlockSpec((tb, PROJ_COLS), lambda i, j: (i, j)),
                   pl.BlockSpec((tb, nba), lambda i, j: (i, 0)),
                   pl.BlockSpec((nba, tb), lambda i, j: (0, i))),
        scratch_shapes=[pltpu.VMEM((tb, d), BF16)],
        compiler_params=_params("arbitrary", "arbitrary"),
        name="proj",
    )(x2, mod3, gain, w_big, w_ba, w_bat)


def _head_norm(v, gain):
    return v * lax.rsqrt(jnp.mean(v * v, axis=-1, keepdims=True) + EPS) * gain


def _attn_kernel(q_ref, kc_ref, vc_ref, kp_ref, vp_ref, qn_ref, kn_ref, sink_ref, o_ref, *, blocks_per_seq):
    blk = WINDOW
    first = (pl.program_id(0) % blocks_per_seq) == 0
    q = q_ref[...].astype(F32)
    k = jnp.concatenate([kp_ref[...], kc_ref[...]], axis=0).astype(F32)
    v = jnp.concatenate([vp_ref[...], vc_ref[...]], axis=0)
    rows = ATTN_GROUP * blk
    qi = lax.broadcasted_iota(jnp.int32, (rows, 2 * blk), 0) & (blk - 1)
    kj = lax.broadcasted_iota(jnp.int32, (rows, 2 * blk), 1)
    rel = qi + blk - kj
    mask = (rel >= 0) & (rel < WINDOW) & (jnp.logical_not(first) | (kj >= blk))
    sinks = sink_ref[...]
    for g in range(ATTN_KV_HEADS):
        lo = g * ATTN_HEAD_DIM
        kg = _head_norm(k[:, lo:lo + ATTN_HEAD_DIM], kn_ref[...]).astype(BF16)
        vg = v[:, lo:lo + ATTN_HEAD_DIM]
        qs, sk = [], []
        for u in range(ATTN_GROUP):
            hh = g * ATTN_GROUP + u
            qs.append(_head_norm(q[:, hh * ATTN_HEAD_DIM:(hh + 1) * ATTN_HEAD_DIM], qn_ref[...]))
            sk.append(jnp.broadcast_to(sinks[0:1, hh:hh + 1], (blk, 1)))
        q4 = jnp.concatenate(qs, axis=0).astype(BF16)
        sink = jnp.concatenate(sk, axis=0)
        s = _mm_nt(q4, kg) * (ATTN_HEAD_DIM ** -0.5)
        s = jnp.where(mask, s, NEG)
        m = jnp.maximum(jnp.max(s, axis=-1, keepdims=True), sink)
        p = jnp.exp(s - m)
        den = jnp.sum(p, axis=-1, keepdims=True) + jnp.exp(sink - m)
        o4 = _mm(p.astype(BF16), vg) / den
        for u in range(ATTN_GROUP):
            hh = g * ATTN_GROUP + u
            o_ref[:, hh * ATTN_HEAD_DIM:(hh + 1) * ATTN_HEAD_DIM] = o4[u * blk:(u + 1) * blk].astype(BF16)


def _attention(proj, qn, kn, sinks, seq):
    t = proj.shape[0]
    blk = WINDOW
    per_seq = seq // blk
    prev = lambda i: jnp.maximum(i - 1, 0)
    return pl.pallas_call(
        functools.partial(_attn_kernel, blocks_per_seq=per_seq),
        out_shape=jax.ShapeDtypeStruct((t, ATTN_Q_WIDTH), BF16),
        grid=(t // blk,),
        in_specs=[pl.BlockSpec((blk, ATTN_Q_WIDTH), lambda i: (i, COL_AQ)),
                  pl.BlockSpec((blk, ATTN_KV_WIDTH), lambda i: (i, COL_AK)),
                  pl.BlockSpec((blk, ATTN_KV_WIDTH), lambda i: (i, COL_AV)),
                  pl.BlockSpec((blk, ATTN_KV_WIDTH), lambda i: (prev(i), COL_AK)),
                  pl.BlockSpec((blk, ATTN_KV_WIDTH), lambda i: (prev(i), COL_AV)),
                  pl.BlockSpec((1, ATTN_HEAD_DIM), lambda i: (0, 0)),
                  pl.BlockSpec((1, ATTN_HEAD_DIM), lambda i: (0, 0)),
                  pl.BlockSpec((1, ATTN_HEADS), lambda i: (0, 0))],
        out_specs=pl.BlockSpec((blk, ATTN_Q_WIDTH), lambda i: (i, 0)),
        compiler_params=_params("arbitrary"),
        name="attn",
    )(proj, proj, proj, proj, proj, qn, kn, sinks)


DN_BLOCK = 128
CONV_HALO = 8


def _dn_kernel(q_ref, k_ref, v_ref, pq_ref, pk_ref, pv_ref, z_ref, ba_ref, bat_ref, cw_ref,
               alog_r, dt_r, alog_c, dt_c, on_ref, o_ref, state, *, blocks_per_seq):
    first = (pl.program_id(0) % blocks_per_seq) == 0
    cb, cl, nh = DN_BLOCK, DN_CHUNK, DN_HEADS

    @pl.when(first)
    def _():
        state[...] = jnp.zeros_like(state)

    def conv(cur_ref, prev_ref, part):
        cur = cur_ref[...].astype(F32)
        prev = jnp.where(first, 0.0, prev_ref[...].astype(F32))
        xc = jnp.concatenate([prev, cur], axis=0)
        acc = None
        for j in range(DN_CONV):
            w = cw_ref[j:j + 1, part * DN_WIDTH:(part + 1) * DN_WIDTH]
            shift = DN_CONV - 1 - j
            xs = pltpu.roll(xc, shift, 0) if shift else xc
            term = w * xs[CONV_HALO:CONV_HALO + cb]
            acc = term if acc is None else acc + term
        return _silu(acc)

    qc, kc, vc = conv(q_ref, pq_ref, 0), conv(k_ref, pk_ref, 1), conv(v_ref, pv_ref, 2)

    ba = ba_ref[...]
    bat = bat_ref[...]
    beta = jax.nn.sigmoid(ba[:, :nh])
    g_col = -jnp.exp(alog_r[...]) * jax.nn.softplus(ba[:, nh:] + dt_r[...])
    g_row = -jnp.exp(alog_c[...]) * jax.nn.softplus(bat[nh:] + dt_c[...])

    r = lax.broadcasted_iota(jnp.int32, (cb, cb), 0)
    c = lax.broadcasted_iota(jnp.int32, (cb, cb), 1)
    same = (r // cl) == (c // cl)
    lower = jnp.where(same & (c <= r), 1.0, 0.0).astype(BF16)
    upper = jnp.where(same & (r <= c), 1.0, 0.0).astype(BF16)

    def exact3(mm, a, b, split_left):
        v0 = a if split_left else b
        p0 = v0.astype(BF16)
        r1 = v0 - p0.astype(F32)
        p1 = r1.astype(BF16)
        p2 = (r1 - p1.astype(F32)).astype(BF16)
        if split_left:
            return mm(p0, b) + (mm(p1, b) + mm(p2, b))
        return mm(a, p0) + (mm(a, p1) + mm(a, p2))

    gc_col = exact3(_mm, lower, g_col, False)
    gc_row = exact3(_mm, g_row, upper, True)

    ri = lax.broadcasted_iota(jnp.int32, (cl, cl), 0)
    ci = lax.broadcasted_iota(jnp.int32, (cl, cl), 1)
    tril = ci <= ri
    strict = ci < ri
    eye = jnp.where(ci == ri, 1.0, 0.0)
    eye_k = jnp.where(lax.broadcasted_iota(jnp.int32, (DN_DIM, DN_DIM), 0)
                      == lax.broadcasted_iota(jnp.int32, (DN_DIM, DN_DIM), 1), 1.0, 0.0).astype(BF16)

    z = z_ref[...].astype(F32)
    for ch in range(cb // cl):
        t0 = ch * cl
        qs, ks, kbs, kgs, vbs, decs, qgs, kds, gls = [], [], [], [], [], [], [], [], []
        for h in range(nh):
            sl = slice(h * DN_DIM, (h + 1) * DN_DIM)
            qh = qc[t0:t0 + cl, sl]
            kh = kc[t0:t0 + cl, sl]
            vh = vc[t0:t0 + cl, sl]
            qh = qh * lax.rsqrt(jnp.sum(qh * qh, axis=-1, keepdims=True) + EPS) * (DN_DIM ** -0.5)
            kh = kh * lax.rsqrt(jnp.sum(kh * kh, axis=-1, keepdims=True) + EPS)
            bh = beta[t0:t0 + cl, h:h + 1]
            gcol = gc_col[t0:t0 + cl, h:h + 1]
            grow = gc_row[h:h + 1, t0:t0 + cl]
            glast = gc_col[t0 + cl - 1:t0 + cl, h:h + 1]
            dec = jnp.where(tril, jnp.exp(jnp.where(tril, gcol - grow, 0.0)), 0.0)
            eg = jnp.exp(gcol)
            kb = kh * bh
            qs.append(qh); ks.append(kh); kbs.append(kb); kgs.append(kb * eg)
            vbs.append(vh * bh); decs.append(dec); qgs.append(qh * eg)
            kds.append(kh * jnp.exp(glast - gcol)); gls.append(jnp.exp(glast))
        q3, k3 = jnp.stack(qs).astype(BF16), jnp.stack(ks).astype(BF16)
        kb3, kg3, vb3 = jnp.stack(kbs).astype(BF16), jnp.stack(kgs).astype(BF16), jnp.stack(vbs).astype(BF16)
        dec3, qg3, kd3 = jnp.stack(decs), jnp.stack(qgs).astype(BF16), jnp.stack(kds).astype(BF16)

        a_low = jnp.where(strict[None], _bmm_nt(kb3, k3) * dec3, 0.0)
        pw = -a_low
        t_mat = eye[None] + pw
        for _ in range(5):
            pw = _three_pass(_bmm, pw, pw)
            t_mat = t_mat + _three_pass(_bmm, t_mat, pw)
        t_bf = t_mat.astype(BF16)
        u3 = _bmm(t_bf, vb3)
        w3 = _bmm(t_bf, kg3)
        qk3 = _bmm_nt(q3, k3) * dec3

        s_old = state[...]
        s_bf = s_old.astype(BF16)
        v_new = u3 - _bmm(w3.astype(BF16), s_bf)
        o3 = _bmm(qg3, s_bf) + _bmm(qk3.astype(BF16), v_new.astype(BF16))
        kdt = _bmm_nt(jnp.broadcast_to(eye_k[None], (nh, DN_DIM, DN_DIM)), kd3)
        upd = _bmm(kdt.astype(BF16), v_new.astype(BF16))
        state[...] = jnp.stack([s_old[h] * gls[h] for h in range(nh)]) + upd

        for h in range(nh):
            sl = slice(h * DN_DIM, (h + 1) * DN_DIM)
            oh = o3[h]
            oh = oh * lax.rsqrt(jnp.mean(oh * oh, axis=-1, keepdims=True) + EPS) * on_ref[...]
            o_ref[t0:t0 + cl, sl] = (oh * _silu(z[t0:t0 + cl, sl])).astype(BF16)


def _deltanet(proj, ba, bat, conv_w, a_log, dt_bias, out_norm, seq):
    t = proj.shape[0]
    cb = DN_BLOCK
    per_seq = seq // cb
    halo = cb // CONV_HALO
    prev = lambda i: jnp.maximum(i * halo - 1, 0)
    nba = ba.shape[1]
    row = lambda a: a.reshape(1, -1)
    col = lambda a: a.reshape(-1, 1)
    small = lambda shp: pl.BlockSpec(shp, lambda i: (0, 0))
    return pl.pallas_call(
        functools.partial(_dn_kernel, blocks_per_seq=per_seq),
        out_shape=jax.ShapeDtypeStruct((t, DN_WIDTH), BF16),
        grid=(t // cb,),
        in_specs=[pl.BlockSpec((cb, DN_WIDTH), lambda i: (i, COL_DQ)),
                  pl.BlockSpec((cb, DN_WIDTH), lambda i: (i, COL_DK)),
                  pl.BlockSpec((cb, DN_WIDTH), lambda i: (i, COL_DV)),
                  pl.BlockSpec((CONV_HALO, DN_WIDTH), lambda i: (prev(i), COL_DQ)),
                  pl.BlockSpec((CONV_HALO, DN_WIDTH), lambda i: (prev(i), COL_DK)),
                  pl.BlockSpec((CONV_HALO, DN_WIDTH), lambda i: (prev(i), COL_DV)),
                  pl.BlockSpec((cb, DN_WIDTH), lambda i: (i, COL_DZ)),
                  pl.BlockSpec((cb, nba), lambda i: (i, 0)),
                  pl.BlockSpec((nba, cb), lambda i: (0, i)),
                  small((DN_CONV, 3 * DN_WIDTH)),
                  small((1, DN_HEADS)), small((1, DN_HEADS)),
                  small((DN_HEADS, 1)), small((DN_HEADS, 1)),
                  small((1, DN_DIM))],
        out_specs=pl.BlockSpec((cb, DN_WIDTH), lambda i: (i, 0)),
        scratch_shapes=[pltpu.VMEM((DN_HEADS, DN_DIM, DN_DIM), F32)],
        compiler_params=_params("arbitrary"),
        name="deltanet",
    )(proj, proj, proj, proj, proj, proj, proj, ba, bat, conv_w,
      row(a_log), row(dt_bias), col(a_log), col(dt_bias), row(out_norm))


def _mix_kernel(att_ref, od_ref, ga_ref, gb_ref, x_ref, mod_ref, wa_ref, wb_ref, wo_ref, g2_ref,
                wqh_ref, wql_ref, x1_ref, h2_ref, q_ref):
    ya = _mm(att_ref[...], wa_ref[...])
    yb = _mm(od_ref[...], wb_ref[...])
    mixed = jax.nn.sigmoid(ga_ref[...].astype(F32)) * ya + jax.nn.sigmoid(gb_ref[...].astype(F32)) * yb
    mod = mod_ref[...]
    x1 = x_ref[...] + mod[2:3] * _mm(mixed.astype(BF16), wo_ref[...])
    x1_ref[...] = x1
    y = x1 * lax.rsqrt(jnp.mean(x1 * x1, axis=-1, keepdims=True) + EPS) * g2_ref[...]
    h2 = y * (1.0 + mod[4:5]) + mod[3:4]
    h2_ref[...] = h2.astype(BF16)
    hh, hl = _split(h2)
    q_ref[...] = _mm(hh, wqh_ref[...]) + (_mm(hl, wqh_ref[...]) + _mm(hh, wql_ref[...]))


def _mix(attn, od, proj, x2, mod3, wa, wb, wo, gain2, wq_hi, wq_lo, seq, tb):
    t, d = x2.shape
    per_seq = seq // tb
    nq = wq_hi.shape[1]
    full = lambda shp: pl.BlockSpec(shp, lambda i: (0, 0))
    return pl.pallas_call(
        _mix_kernel,
        out_shape=(jax.ShapeDtypeStruct((t, d), F32),
                   jax.ShapeDtypeStruct((t, d), BF16),
                   jax.ShapeDtypeStruct((t, nq), F32)),
        grid=(t // tb,),
        in_specs=[pl.BlockSpec((tb, d), lambda i: (i, 0)),
                  pl.BlockSpec((tb, d), lambda i: (i, 0)),
                  pl.BlockSpec((tb, d), lambda i: (i, COL_GA)),
                  pl.BlockSpec((tb, d), lambda i: (i, COL_GB)),
                  pl.BlockSpec((tb, d), lambda i: (i, 0)),
                  pl.BlockSpec((None, 6, d), lambda i: (i // per_seq, 0, 0)),
                  full((d, d)), full((d, d)), full((d, d)), full((1, d)),
                  full((d, nq)), full((d, nq))],
        out_specs=(pl.BlockSpec((tb, d), lambda i: (i, 0)),
                   pl.BlockSpec((tb, d), lambda i: (i, 0)),
                   pl.BlockSpec((tb, nq), lambda i: (i, 0))),
        compiler_params=_params("arbitrary"),
        name="mix",
    )(attn, od, proj, proj, x2, mod3, wa, wb, wo, gain2, wq_hi, wq_lo)


_CAND_ROWS = tuple(PEER_TOPK // (a + 1) for a in range(PEER_TOPK))


def _top16(s, iota):
    n = s.shape[0]
    rank = jnp.full(s.shape, float(PEER_TOPK), F32)
    vals = []
    for r in range(PEER_TOPK):
        m = jnp.max(s, axis=0, keepdims=True)
        idx = jnp.min(jnp.where(s == m, iota, n), axis=0, keepdims=True)
        sel = iota == idx
        rank = jnp.where(sel, float(r), rank)
        s = jnp.where(sel, NEG_INF, s)
        vals.append(m)
    return vals, rank


def _route_kernel(q_ref, keys_ref, a_ref, n1_ref, b_ref, r2_ref, ts_scr):
    tb = q_ref.shape[0]
    nk = PEER_N_KEYS
    iota = lax.broadcasted_iota(jnp.int32, (nk, tb), 0)
    sub = lax.broadcasted_iota(jnp.int32, (PEER_TOPK, tb), 0)
    for h in range(PEER_HEADS):
        sc, tops, ranks = [], [], []
        for p in range(2):
            hp = 2 * h + p
            s = _three_pass(_mm_nt, keys_ref[hp], q_ref[:, hp * PEER_HALF:(hp + 1) * PEER_HALF])
            vals, rank = _top16(s, iota)
            sc.append(s); tops.append(vals); ranks.append(rank)
        for r in range(PEER_TOPK):
            ts_scr[r:r + 1, :] = tops[1][r]
        ts2 = ts_scr[...]
        cand, pos = [], []
        for a in range(PEER_TOPK):
            ok = sub < _CAND_ROWS[a]
            cand.append(jnp.where(ok, tops[0][a] + ts2, NEG_INF))
            pos.append(sub + a * PEER_TOPK)
        big = PEER_TOPK * PEER_TOPK
        m0 = tops[0][0] + tops[1][0]
        zsum = jnp.zeros_like(m0)
        for r in range(PEER_TOPK):
            m = functools.reduce(jnp.maximum, [jnp.max(cv, axis=0, keepdims=True) for cv in cand])
            first = functools.reduce(jnp.minimum, [jnp.min(jnp.where(cv == m, pv, big), axis=0, keepdims=True)
                                                   for cv, pv in zip(cand, pos)])
            cand = [jnp.where(pv == first, NEG_INF, cv) for cv, pv in zip(cand, pos)]
            zsum = zsum + jnp.exp(m - m0)
        n1 = jnp.zeros((nk, tb), F32)
        for a in range(PEER_TOPK):
            taken = jnp.sum(jnp.where((sub < _CAND_ROWS[a]) & (cand[a] == NEG_INF), 1.0, 0.0), axis=0, keepdims=True)
            n1 = jnp.where(ranks[0] == float(a), taken, n1)
        in1 = ranks[0] < float(PEER_TOPK)
        in2 = ranks[1] < float(PEER_TOPK)
        a_ref[h] = jnp.where(in1, jnp.exp(jnp.where(in1, sc[0] - tops[0][0], 0.0)), 0.0)
        n1_ref[h] = n1
        b_ref[h] = jnp.where(in2, jnp.exp(jnp.where(in2, sc[1] - tops[1][0], 0.0)) / zsum, 0.0)
        r2_ref[h] = ranks[1]


def _route(q, keys, tb):
    t = q.shape[0]
    shp = jax.ShapeDtypeStruct((PEER_HEADS, PEER_N_KEYS, t), F32)
    ospec = pl.BlockSpec((PEER_HEADS, PEER_N_KEYS, tb), lambda i: (0, 0, i))
    return pl.pallas_call(
        _route_kernel,
        out_shape=(shp, shp, shp, shp),
        grid=(t // tb,),
        in_specs=[pl.BlockSpec((tb, q.shape[1]), lambda i: (i, 0)),
                  pl.BlockSpec(keys.shape, lambda i: (0, 0, 0))],
        out_specs=(ospec, ospec, ospec, ospec),
        scratch_shapes=[pltpu.VMEM((PEER_TOPK, tb), F32)],
        compiler_params=_params("arbitrary"),
        name="route",
    )(q, keys)


EXPERT_ROWS = 8


def _experts_kernel(h2_ref, u_ref, vt_ref, a_ref, n1_ref, b_ref, r2_ref, x1_ref, mod_ref, o_ref, acc, act):
    e = pl.program_id(1)

    @pl.when(e == 0)
    def _():
        acc[...] = jnp.zeros_like(acc)

    xb = h2_ref[...]
    nk = PEER_N_KEYS
    for ii in range(EXPERT_ROWS):
        hid = _mm_nt(u_ref[ii * nk:(ii + 1) * nk, :], xb)
        gate = None
        for h in range(PEER_HEADS):
            term = jnp.where(r2_ref[h] < n1_ref[h, ii:ii + 1, :], b_ref[h], 0.0) * a_ref[h, ii:ii + 1, :]
            gate = term if gate is None else gate + term
        act[ii * nk:(ii + 1) * nk, :] = (_gelu(hid) * gate).astype(BF16)
    acc[...] += _mm(vt_ref[...], act[...])

    @pl.when(e == pl.num_programs(1) - 1)
    def _():
        o_ref[...] = x1_ref[...] + mod_ref[5:6, :] * acc[...].T


def _experts(h2, u_bf, vt_bf, a, n1, b, r2, x1, mod3, seq, tb):
    t, d = h2.shape
    per_seq = seq // tb
    ne = u_bf.shape[0]
    eb = EXPERT_ROWS * PEER_N_KEYS
    rows = pl.BlockSpec((PEER_HEADS, EXPERT_ROWS, tb), lambda i, e: (0, e, i))
    allk = pl.BlockSpec((PEER_HEADS, PEER_N_KEYS, tb), lambda i, e: (0, 0, i))
    return pl.pallas_call(
        _experts_kernel,
        out_shape=jax.ShapeDtypeStruct((t, d), F32),
        grid=(t // tb, ne // eb),
        in_specs=[pl.BlockSpec((tb, d), lambda i, e: (i, 0)),
                  pl.BlockSpec((eb, d), lambda i, e: (e, 0)),
                  pl.BlockSpec((d, eb), lambda i, e: (0, e)),
                  rows, rows, allk, allk,
                  pl.BlockSpec((tb, d), lambda i, e: (i, 0)),
                  pl.BlockSpec((None, 6, d), lambda i, e: (i // per_seq, 0, 0))],
        out_specs=pl.BlockSpec((tb, d), lambda i, e: (i, 0)),
        scratch_shapes=[pltpu.VMEM((d, tb), F32), pltpu.VMEM((eb, tb), BF16)],
        compiler_params=_params("arbitrary", "arbitrary"),
        name="experts",
    )(h2, u_bf, vt_bf, a, n1, b, r2, x1, mod3)


def _regroup_w_in(w):
    d = D_MODEL
    o = 0
    parts = {}
    for name, width in (("aq", ATTN_Q_WIDTH), ("ak", ATTN_KV_WIDTH), ("av", ATTN_KV_WIDTH), ("dq", DN_WIDTH),
                        ("dk", DN_WIDTH), ("dv", DN_WIDTH), ("dz", DN_WIDTH), ("ba", 2 * DN_HEADS),
                        ("ga", d), ("gb", d)):
        parts[name] = w[:, o:o + width]
        o += width
    big = jnp.concatenate([parts[n] for n in ("aq", "dq", "dk", "dv", "dz", "ga", "gb", "ak", "av")], axis=1)
    return big.astype(BF16), parts["ba"], parts["ba"].T


def _layer(x2, c, seq, w_ada, b_ada, norm1_gain, w_in, attn_q_norm, attn_k_norm, attn_sinks, dn_conv_w, dn_a_log,
           dn_dt_bias, dn_out_norm, w_attn_branch, w_dn_branch, w_out, norm2_gain, peer_w_q, peer_sub_keys,
           peer_u, peer_v):
    d = D_MODEL
    row = lambda a: a.reshape(1, -1)
    tb = min(512, seq)

    mod3 = _adaln(c, w_ada, b_ada).reshape(c.shape[0], 6, d)
    w_big, w_ba, w_bat = _regroup_w_in(w_in)
    proj, ba, bat = _proj(x2, mod3, row(norm1_gain), w_big, w_ba, w_bat, seq, tb)
    attn = _attention(proj, row(attn_q_norm), row(attn_k_norm), row(attn_sinks), seq)
    od = _deltanet(proj, ba, bat, dn_conv_w, dn_a_log, dn_dt_bias, dn_out_norm, seq)
    wq_hi = peer_w_q.astype(BF16)
    wq_lo = (peer_w_q - wq_hi.astype(F32)).astype(BF16)
    x1, h2, q = _mix(attn, od, proj, x2, mod3, w_attn_branch.astype(BF16), w_dn_branch.astype(BF16),
                     w_out.astype(BF16), row(norm2_gain), wq_hi, wq_lo, seq, tb)
    keys = peer_sub_keys.reshape(2 * PEER_HEADS, PEER_N_KEYS, PEER_HALF)
    a, n1, b, r2 = _route(q, keys, min(256, seq))
    return _experts(h2, peer_u.astype(BF16), peer_v.T.astype(BF16), a, n1, b, r2, x1, mod3, seq, tb)


def kernel(x, c, w_ada, b_ada, norm1_gain, w_in, attn_q_norm, attn_k_norm, attn_sinks, dn_conv_w, dn_a_log, dn_dt_bias, dn_out_norm, w_attn_branch, w_dn_branch, w_out, norm2_gain, peer_w_q, peer_sub_keys, peer_u, peer_v):
    b, s, d = x.shape
    x2 = x.reshape(b * s, d)
    for l in range(w_ada.shape[0]):
        x2 = _layer(x2, c, s, w_ada[l], b_ada[l], norm1_gain[l], w_in[l], attn_q_norm[l], attn_k_norm[l],
                    attn_sinks[l], dn_conv_w[l], dn_a_log[l], dn_dt_bias[l], dn_out_norm[l], w_attn_branch[l],
                    w_dn_branch[l], w_out[l], norm2_gain[l], peer_w_q[l], peer_sub_keys[l], peer_u[l], peer_v[l])
    return x2.reshape(b, s, d)
```

```python
import functools

import jax
import jax.numpy as jnp
import numpy as np
from jax import lax
from jax.experimental import pallas as pl
from jax.experimental.pallas import tpu as pltpu

F32 = jnp.float32
BF16 = jnp.bfloat16

D_MODEL = 1024
ATTN_HEADS = 16
ATTN_KV_HEADS = 4
ATTN_HEAD_DIM = 64
ATTN_GROUP = ATTN_HEADS // ATTN_KV_HEADS
ATTN_Q_WIDTH = ATTN_HEADS * ATTN_HEAD_DIM
ATTN_KV_WIDTH = ATTN_KV_HEADS * ATTN_HEAD_DIM
WINDOW = 128
DN_HEADS = 8
DN_DIM = 128
DN_WIDTH = DN_HEADS * DN_DIM
DN_CONV = 4
DN_CHUNK = 64
PEER_HEADS = 8
PEER_N_KEYS = 128
PEER_HALF = 128
PEER_TOPK = 16
EPS = 1e-6
NEG = -1e30
NEG_INF = float("-inf")

VMEM_LIMIT = 48 * 1024 * 1024


def _params(*sem):
    return pltpu.CompilerParams(dimension_semantics=sem, vmem_limit_bytes=VMEM_LIMIT)


def _mm(a, b):
    return jnp.dot(a, b, preferred_element_type=F32)


def _mm_nt(a, b):
    return lax.dot_general(a, b, (((1,), (1,)), ((), ())), preferred_element_type=F32)


def _bmm(a, b):
    return lax.dot_general(a, b, (((2,), (1,)), ((0,), (0,))), preferred_element_type=F32)


def _bmm_nt(a, b):
    return lax.dot_general(a, b, (((2,), (2,)), ((0,), (0,))), preferred_element_type=F32)


def _split(a):
    hi = a.astype(BF16)
    lo = (a - hi.astype(F32)).astype(BF16)
    return hi, lo


def _three_pass(mm, a, b):
    ah, al = _split(a)
    bh, bl = _split(b)
    return mm(ah, bh) + (mm(al, bh) + mm(ah, bl))


def _silu(v):
    return v * jax.nn.sigmoid(v)


def _gelu(v):
    return 0.5 * v * (1.0 + lax.erf(v * (2.0 ** -0.5)))


def _adaln_kernel(c_ref, w_ref, b_ref, o_ref):
    o_ref[...] = _three_pass(_mm, _silu(c_ref[...]), w_ref[...]) + b_ref[...]


def _adaln(c, w_ada, b_ada):
    b, d = c.shape
    n = w_ada.shape[1] // d
    return pl.pallas_call(
        _adaln_kernel,
        out_shape=jax.ShapeDtypeStruct((b, n * d), F32),
        grid=(n,),
        in_specs=[pl.BlockSpec((b, d), lambda j: (0, 0)),
                  pl.BlockSpec((d, d), lambda j: (0, j)),
                  pl.BlockSpec((1, d), lambda j: (0, j))],
        out_specs=pl.BlockSpec((b, d), lambda j: (0, j)),
        compiler_params=_params("arbitrary"),
        name="adaln",
    )(c, w_ada, b_ada.reshape(1, -1))


PROJ_COLS = 512
COL_AQ, COL_DQ, COL_DK, COL_DV, COL_DZ, COL_GA, COL_GB = range(7)
COL_AK = 7 * D_MODEL // ATTN_KV_WIDTH
COL_AV = COL_AK + 1
PROJ_WIDTH = 7 * D_MODEL + 2 * ATTN_KV_WIDTH


def _proj_kernel(x_ref, mod_ref, g_ref, w_ref, wba_ref, wbat_ref, o_ref, ba_ref, bat_ref, h_scr):
    @pl.when(pl.program_id(1) == 0)
    def _():
        x = x_ref[...]
        y = x * lax.rsqrt(jnp.mean(x * x, axis=-1, keepdims=True) + EPS) * g_ref[...]
        mod = mod_ref[...]
        h = y * (1.0 + mod[1:2]) + mod[0:1]
        h_scr[...] = h.astype(BF16)
        ba_ref[...] = _three_pass(_mm, h, wba_ref[...])
        bat_ref[...] = _three_pass(_mm_nt, wbat_ref[...], h)

    o_ref[...] = _mm(h_scr[...], w_ref[...]).astype(BF16)


def _proj(x2, mod3, gain, w_big, w_ba, w_bat, seq, tb):
    t, d = x2.shape
    per_seq = seq // tb
    nba = w_ba.shape[1]
    return pl.pallas_call(
        _proj_kernel,
        out_shape=(jax.ShapeDtypeStruct((t, PROJ_WIDTH), BF16),
                   jax.ShapeDtypeStruct((t, nba), F32),
                   jax.ShapeDtypeStruct((nba, t), F32)),
        grid=(t // tb, PROJ_WIDTH // PROJ_COLS),
        in_specs=[pl.BlockSpec((tb, d), lambda i, j: (i, 0)),
                  pl.BlockSpec((None, 6, d), lambda i, j: (i // per_seq, 0, 0)),
                  pl.BlockSpec((1, d), lambda i, j: (0, 0)),
                  pl.BlockSpec((d, PROJ_COLS), lambda i, j: (0, j)),
                  pl.BlockSpec((d, nba), lambda i, j: (0, 0)),
                  pl.BlockSpec((nba, d), lambda i, j: (0, 0))],
        out_specs=(pl.BlockSpec((tb, PROJ_COLS), lambda i, j: (i, j)),
                   pl.BlockSpec((tb, nba), lambda i, j: (i, 0)),
                   pl.BlockSpec((nba, tb), lambda i, j: (0, i))),
        scratch_shapes=[pltpu.VMEM((tb, d), BF16)],
        compiler_params=_params("arbitrary", "arbitrary"),
        name="proj",
    )(x2, mod3, gain, w_big, w_ba, w_bat)


def _head_norm(v, gain):
    return v * lax.rsqrt(jnp.mean(v * v, axis=-1, keepdims=True) + EPS) * gain


def _attn_kernel(q_ref, kc_ref, vc_ref, kp_ref, vp_ref, qn_ref, kn_ref, sink_ref, o_ref, *, blocks_per_seq):
    blk = WINDOW
    first = (pl.program_id(0) % blocks_per_seq) == 0
    q = q_ref[...].astype(F32)
    k = jnp.concatenate([kp_ref[...], kc_ref[...]], axis=0).astype(F32)
    v = jnp.concatenate([vp_ref[...], vc_ref[...]], axis=0)
    rows = ATTN_GROUP * blk
    qi = lax.broadcasted_iota(jnp.int32, (rows, 2 * blk), 0) & (blk - 1)
    kj = lax.broadcasted_iota(jnp.int32, (rows, 2 * blk), 1)
    rel = qi + blk - kj
    mask = (rel >= 0) & (rel < WINDOW) & (jnp.logical_not(first) | (kj >= blk))
    sinks = sink_ref[...]
    for g in range(ATTN_KV_HEADS):
        lo = g * ATTN_HEAD_DIM
        kg = _head_norm(k[:, lo:lo + ATTN_HEAD_DIM], kn_ref[...]).astype(BF16)
        vg = v[:, lo:lo + ATTN_HEAD_DIM]
        qs, sk = [], []
        for u in range(ATTN_GROUP):
            hh = g * ATTN_GROUP + u
            qs.append(_head_norm(q[:, hh * ATTN_HEAD_DIM:(hh + 1) * ATTN_HEAD_DIM], qn_ref[...]))
            sk.append(jnp.broadcast_to(sinks[0:1, hh:hh + 1], (blk, 1)))
        q4 = jnp.concatenate(qs, axis=0).astype(BF16)
        sink = jnp.concatenate(sk, axis=0)
        s = _mm_nt(q4, kg) * (ATTN_HEAD_DIM ** -0.5)
        s = jnp.where(mask, s, NEG)
        m = jnp.maximum(jnp.max(s, axis=-1, keepdims=True), sink)
        p = jnp.exp(s - m)
        den = jnp.sum(p, axis=-1, keepdims=True) + jnp.exp(sink - m)
        o4 = _mm(p.astype(BF16), vg) / den
        for u in range(ATTN_GROUP):
            hh = g * ATTN_GROUP + u
            o_ref[:, hh * ATTN_HEAD_DIM:(hh + 1) * ATTN_HEAD_DIM] = o4[u * blk:(u + 1) * blk].astype(BF16)


def _attention(proj, qn, kn, sinks, seq):
    t = proj.shape[0]
    blk = WINDOW
    per_seq = seq // blk
    prev = lambda i: jnp.maximum(i - 1, 0)
    return pl.pallas_call(
        functools.partial(_attn_kernel, blocks_per_seq=per_seq),
        out_shape=jax.ShapeDtypeStruct((t, ATTN_Q_WIDTH), BF16),
        grid=(t // blk,),
        in_specs=[pl.BlockSpec((blk, ATTN_Q_WIDTH), lambda i: (i, COL_AQ)),
                  pl.BlockSpec((blk, ATTN_KV_WIDTH), lambda i: (i, COL_AK)),
                  pl.BlockSpec((blk, ATTN_KV_WIDTH), lambda i: (i, COL_AV)),
                  pl.BlockSpec((blk, ATTN_KV_WIDTH), lambda i: (prev(i), COL_AK)),
                  pl.BlockSpec((blk, ATTN_KV_WIDTH), lambda i: (prev(i), COL_AV)),
                  pl.BlockSpec((1, ATTN_HEAD_DIM), lambda i: (0, 0)),
                  pl.BlockSpec((1, ATTN_HEAD_DIM), lambda i: (0, 0)),
                  pl.BlockSpec((1, ATTN_HEADS), lambda i: (0, 0))],
        out_specs=pl.BlockSpec((blk, ATTN_Q_WIDTH), lambda i: (i, 0)),
        compiler_params=_params("arbitrary"),
        name="attn",
    )(proj, proj, proj, proj, proj, qn, kn, sinks)


DN_BLOCK = 128
CONV_HALO = 8


def _dn_kernel(q_ref, k_ref, v_ref, pq_ref, pk_ref, pv_ref, z_ref, ba_ref, bat_ref, cw_ref,
               alog_r, dt_r, alog_c, dt_c, on_ref, o_ref, state, *, blocks_per_seq):
    first = (pl.program_id(0) % blocks_per_seq) == 0
    cb, cl, nh = DN_BLOCK, DN_CHUNK, DN_HEADS

    @pl.when(first)
    def _():
        state[...] = jnp.zeros_like(state)

    def conv(cur_ref, prev_ref, part):
        cur = cur_ref[...].astype(F32)
        prev = jnp.where(first, 0.0, prev_ref[...].astype(F32))
        xc = jnp.concatenate([prev, cur], axis=0)
        acc = None
        for j in range(DN_CONV):
            w = cw_ref[j:j + 1, part * DN_WIDTH:(part + 1) * DN_WIDTH]
            shift = DN_CONV - 1 - j
            xs = pltpu.roll(xc, shift, 0) if shift else xc
            term = w * xs[CONV_HALO:CONV_HALO + cb]
            acc = term if acc is None else acc + term
        return _silu(acc)

    qc, kc, vc = conv(q_ref, pq_ref, 0), conv(k_ref, pk_ref, 1), conv(v_ref, pv_ref, 2)

    ba = ba_ref[...]
    bat = bat_ref[...]
    beta = jax.nn.sigmoid(ba[:, :nh])
    g_col = -jnp.exp(alog_r[...]) * jax.nn.softplus(ba[:, nh:] + dt_r[...])
    g_row = -jnp.exp(alog_c[...]) * jax.nn.softplus(bat[nh:] + dt_c[...])

    r = lax.broadcasted_iota(jnp.int32, (cb, cb), 0)
    c = lax.broadcasted_iota(jnp.int32, (cb, cb), 1)
    same = (r // cl) == (c // cl)
    lower = jnp.where(same & (c <= r), 1.0, 0.0).astype(BF16)
    upper = jnp.where(same & (r <= c), 1.0, 0.0).astype(BF16)

    def exact3(mm, a, b, split_left):
        v0 = a if split_left else b
        p0 = v0.astype(BF16)
        r1 = v0 - p0.astype(F32)
        p1 = r1.astype(BF16)
        p2 = (r1 - p1.astype(F32)).astype(BF16)
        if split_left:
            return mm(p0, b) + (mm(p1, b) + mm(p2, b))
        return mm(a, p0) + (mm(a, p1) + mm(a, p2))

    gc_col = exact3(_mm, lower, g_col, False)
    gc_row = exact3(_mm, g_row, upper, True)

    ri = lax.broadcasted_iota(jnp.int32, (cl, cl), 0)
    ci = lax.broadcasted_iota(jnp.int32, (cl, cl), 1)
    tril = ci <= ri
    strict = ci < ri
    eye = jnp.where(ci == ri, 1.0, 0.0)
    eye_k = jnp.where(lax.broadcasted_iota(jnp.int32, (DN_DIM, DN_DIM), 0)
                      == lax.broadcasted_iota(jnp.int32, (DN_DIM, DN_DIM), 1), 1.0, 0.0).astype(BF16)

    z = z_ref[...].astype(F32)
    for ch in range(cb // cl):
        t0 = ch * cl
        qs, ks, kbs, kgs, vbs, decs, qgs, kds, gls = [], [], [], [], [], [], [], [], []
        for h in range(nh):
            sl = slice(h * DN_DIM, (h + 1) * DN_DIM)
            qh = qc[t0:t0 + cl, sl]
            kh = kc[t0:t0 + cl, sl]
            vh = vc[t0:t0 + cl, sl]
            qh = qh * lax.rsqrt(jnp.sum(qh * qh, axis=-1, keepdims=True) + EPS) * (DN_DIM ** -0.5)
            kh = kh * lax.rsqrt(jnp.sum(kh * kh, axis=-1, keepdims=True) + EPS)
            bh = beta[t0:t0 + cl, h:h + 1]
            gcol = gc_col[t0:t0 + cl, h:h + 1]
            grow = gc_row[h:h + 1, t0:t0 + cl]
            glast = gc_col[t0 + cl - 1:t0 + cl, h:h + 1]
            dec = jnp.where(tril, jnp.exp(jnp.where(tril, gcol - grow, 0.0)), 0.0)
            eg = jnp.exp(gcol)
            kb = kh * bh
            qs.append(qh); ks.append(kh); kbs.append(kb); kgs.append(kb * eg)
            vbs.append(vh * bh); decs.append(dec); qgs.append(qh * eg)
            kds.append(kh * jnp.exp(glast - gcol)); gls.append(jnp.exp(glast))
        q3, k3 = jnp.stack(qs).astype(BF16), jnp.stack(ks).astype(BF16)
        kb3, kg3, vb3 = jnp.stack(kbs).astype(BF16), jnp.stack(kgs).astype(BF16), jnp.stack(vbs).astype(BF16)
        dec3, qg3, kd3 = jnp.stack(decs), jnp.stack(qgs).astype(BF16), jnp.stack(kds).astype(BF16)

        a_low = jnp.where(strict[None], _bmm_nt(kb3, k3) * dec3, 0.0)
        pw = -a_low
        t_mat = eye[None] + pw
        for _ in range(5):
            pw = _three_pass(_bmm, pw, pw)
            t_mat = t_mat + _three_pass(_bmm, t_mat, pw)
        t_bf = t_mat.astype(BF16)
        u3 = _bmm(t_bf, vb3)
        w3 = _bmm(t_bf, kg3)
        qk3 = _bmm_nt(q3, k3) * dec3

        s_old = state[...]
        s_bf = s_old.astype(BF16)
        v_new = u3 - _bmm(w3.astype(BF16), s_bf)
        o3 = _bmm(qg3, s_bf) + _bmm(qk3.astype(BF16), v_new.astype(BF16))
        kdt = _bmm_nt(jnp.broadcast_to(eye_k[None], (nh, DN_DIM, DN_DIM)), kd3)
        upd = _bmm(kdt.astype(BF16), v_new.astype(BF16))
        state[...] = jnp.stack([s_old[h] * gls[h] for h in range(nh)]) + upd

        for h in range(nh):
            sl = slice(h * DN_DIM, (h + 1) * DN_DIM)
            oh = o3[h]
            oh = oh * lax.rsqrt(jnp.mean(oh * oh, axis=-1, keepdims=True) + EPS) * on_ref[...]
            o_ref[t0:t0 + cl, sl] = (oh * _silu(z[t0:t0 + cl, sl])).astype(BF16)


def _deltanet(proj, ba, bat, conv_w, a_log, dt_bias, out_norm, seq):
    t = proj.shape[0]
    cb = DN_BLOCK
    per_seq = seq // cb
    halo = cb // CONV_HALO
    prev = lambda i: jnp.maximum(i * halo - 1, 0)
    nba = ba.shape[1]
    row = lambda a: a.reshape(1, -1)
    col = lambda a: a.reshape(-1, 1)
    small = lambda shp: pl.BlockSpec(shp, lambda i: (0, 0))
    return pl.pallas_call(
        functools.partial(_dn_kernel, blocks_per_seq=per_seq),
        out_shape=jax.ShapeDtypeStruct((t, DN_WIDTH), BF16),
        grid=(t // cb,),
        in_specs=[pl.BlockSpec((cb, DN_WIDTH), lambda i: (i, COL_DQ)),
                  pl.BlockSpec((cb, DN_WIDTH), lambda i: (i, COL_DK)),
                  pl.BlockSpec((cb, DN_WIDTH), lambda i: (i, COL_DV)),
                  pl.BlockSpec((CONV_HALO, DN_WIDTH), lambda i: (prev(i), COL_DQ)),
                  pl.BlockSpec((CONV_HALO, DN_WIDTH), lambda i: (prev(i), COL_DK)),
                  pl.BlockSpec((CONV_HALO, DN_WIDTH), lambda i: (prev(i), COL_DV)),
                  pl.BlockSpec((cb, DN_WIDTH), lambda i: (i, COL_DZ)),
                  pl.BlockSpec((cb, nba), lambda i: (i, 0)),
                  pl.BlockSpec((nba, cb), lambda i: (0, i)),
                  small((DN_CONV, 3 * DN_WIDTH)),
                  small((1, DN_HEADS)), small((1, DN_HEADS)),
                  small((DN_HEADS, 1)), small((DN_HEADS, 1)),
                  small((1, DN_DIM))],
        out_specs=pl.BlockSpec((cb, DN_WIDTH), lambda i: (i, 0)),
        scratch_shapes=[pltpu.VMEM((DN_HEADS, DN_DIM, DN_DIM), F32)],
        compiler_params=_params("arbitrary"),
        name="deltanet",
    )(proj, proj, proj, proj, proj, proj, proj, ba, bat, conv_w,
      row(a_log), row(dt_bias), col(a_log), col(dt_bias), row(out_norm))


def _mix_kernel(att_ref, od_ref, ga_ref, gb_ref, x_ref, mod_ref, wa_ref, wb_ref, wo_ref, g2_ref,
                wqh_ref, wql_ref, x1_ref, h2_ref, q_ref):
    ya = _mm(att_ref[...], wa_ref[...])
    yb = _mm(od_ref[...], wb_ref[...])
    mixed = jax.nn.sigmoid(ga_ref[...].astype(F32)) * ya + jax.nn.sigmoid(gb_ref[...].astype(F32)) * yb
    mod = mod_ref[...]
    x1 = x_ref[...] + mod[2:3] * _mm(mixed.astype(BF16), wo_ref[...])
    x1_ref[...] = x1
    y = x1 * lax.rsqrt(jnp.mean(x1 * x1, axis=-1, keepdims=True) + EPS) * g2_ref[...]
    h2 = y * (1.0 + mod[4:5]) + mod[3:4]
    h2_ref[...] = h2.astype(BF16)
    hh, hl = _split(h2)
    q_ref[...] = _mm(hh, wqh_ref[...]) + (_mm(hl, wqh_ref[...]) + _mm(hh, wql_ref[...]))


def _mix(attn, od, proj, x2, mod3, wa, wb, wo, gain2, wq_hi, wq_lo, seq, tb):
    t, d = x2.shape
    per_seq = seq // tb
    nq = wq_hi.shape[1]
    full = lambda shp: pl.BlockSpec(shp, lambda i: (0, 0))
    return pl.pallas_call(
        _mix_kernel,
        out_shape=(jax.ShapeDtypeStruct((t, d), F32),
                   jax.ShapeDtypeStruct((t, d), BF16),
                   jax.ShapeDtypeStruct((t, nq), F32)),
        grid=(t // tb,),
        in_specs=[pl.BlockSpec((tb, d), lambda i: (i, 0)),
                  pl.BlockSpec((tb, d), lambda i: (i, 0)),
                  pl.BlockSpec((tb, d), lambda i: (i, COL_GA)),
                  pl.BlockSpec((tb, d), lambda i: (i, COL_GB)),
                  pl.BlockSpec((tb, d), lambda i: (i, 0)),
                  pl.BlockSpec((None, 6, d), lambda i: (i // per_seq, 0, 0)),
                  full((d, d)), full((d, d)), full((d, d)), full((1, d)),
                  full((d, nq)), full((d, nq))],
        out_specs=(pl.BlockSpec((tb, d), lambda i: (i, 0)),
                   pl.BlockSpec((tb, d), lambda i: (i, 0)),
                   pl.BlockSpec((tb, nq), lambda i: (i, 0))),
        compiler_params=_params("arbitrary"),
        name="mix",
    )(attn, od, proj, proj, x2, mod3, wa, wb, wo, gain2, wq_hi, wq_lo)


ROUTE_DTYPE = BF16
ROUTE_TB = 512

_CAND_ROWS = tuple(PEER_TOPK // (a + 1) for a in range(PEER_TOPK))


def _top16(s, iota):
    n = s.shape[0]
    rank = jnp.full(s.shape, float(PEER_TOPK), F32)
    vals = []
    for r in range(PEER_TOPK):
        m = jnp.max(s, axis=0, keepdims=True)
        idx = jnp.min(jnp.where(s == m, iota, n), axis=0, keepdims=True)
        sel = iota == idx
        rank = jnp.where(sel, float(r), rank)
        s = jnp.where(sel, NEG_INF, s)
        vals.append(m)
    return vals, rank


def _scores(q_ref, keys_ref, hp):
    return _three_pass(_mm_nt, keys_ref[hp], q_ref[:, hp * PEER_HALF:(hp + 1) * PEER_HALF])


def _route_exact(q_ref, keys_ref, a_ref, n1_ref, b_ref, r2_ref, ts_scr):
    tb = q_ref.shape[0]
    nk = PEER_N_KEYS
    iota = lax.broadcasted_iota(jnp.int32, (nk, tb), 0)
    sub = lax.broadcasted_iota(jnp.int32, (PEER_TOPK, tb), 0)
    for h in range(PEER_HEADS):
        sc, tops, ranks = [], [], []
        for p in range(2):
            s = _scores(q_ref, keys_ref, 2 * h + p)
            vals, rank = _top16(s, iota)
            sc.append(s); tops.append(vals); ranks.append(rank)
        for r in range(PEER_TOPK):
            ts_scr[1, r:r + 1, :] = tops[1][r]
        ts2 = ts_scr[1]
        cand, pos = [], []
        for a in range(PEER_TOPK):
            ok = sub < _CAND_ROWS[a]
            cand.append(jnp.where(ok, tops[0][a] + ts2, NEG_INF))
            pos.append(sub + a * PEER_TOPK)
        big = PEER_TOPK * PEER_TOPK
        m0 = tops[0][0] + tops[1][0]
        zsum = jnp.zeros_like(m0)
        for r in range(PEER_TOPK):
            m = functools.reduce(jnp.maximum, [jnp.max(cv, axis=0, keepdims=True) for cv in cand])
            first = functools.reduce(jnp.minimum, [jnp.min(jnp.where(cv == m, pv, big), axis=0, keepdims=True)
                                                   for cv, pv in zip(cand, pos)])
            cand = [jnp.where(pv == first, NEG_INF, cv) for cv, pv in zip(cand, pos)]
            zsum = zsum + jnp.exp(m - m0)
        n1 = jnp.zeros((nk, tb), F32)
        for a in range(PEER_TOPK):
            taken = jnp.sum(jnp.where((sub < _CAND_ROWS[a]) & (cand[a] == NEG_INF), 1.0, 0.0), axis=0, keepdims=True)
            n1 = jnp.where(ranks[0] == float(a), taken, n1)
        in1 = ranks[0] < float(PEER_TOPK)
        in2 = ranks[1] < float(PEER_TOPK)
        a_ref[h] = jnp.where(in1, jnp.exp(jnp.where(in1, sc[0] - tops[0][0], 0.0)), 0.0)
        n1_ref[h] = n1
        b_ref[h] = jnp.where(in2, jnp.exp(jnp.where(in2, sc[1] - tops[1][0], 0.0)) / zsum, 0.0).astype(ROUTE_DTYPE)
        r2_ref[h] = ranks[1].astype(ROUTE_DTYPE)


_CAND_PAIRS = tuple((a, b) for a in range(PEER_TOPK) for b in range(_CAND_ROWS[a]))
_CAND_PAD = -len(_CAND_PAIRS) % 8
_CAND_N = len(_CAND_PAIRS) + _CAND_PAD


def _select_rows(onehot, vals):
    p0 = vals.astype(BF16)
    r1 = vals - p0.astype(F32)
    p1 = r1.astype(BF16)
    p2 = (r1 - p1.astype(F32)).astype(BF16)
    return _mm(onehot, p0) + (_mm(onehot, p1) + _mm(onehot, p2))


def _route_distinct(q_ref, keys_ref, sel1_ref, sel2_ref, cnt_ref, pad_ref, a_ref, n1_ref, b_ref, r2_ref, ts_scr):
    tb = q_ref.shape[0]
    topk = float(PEER_TOPK)
    bad = jnp.zeros((1, tb), F32)
    for h in range(PEER_HEADS):
        sc = [_scores(q_ref, keys_ref, 2 * h + p) for p in range(2)]
        rank2 = jnp.full(sc[1].shape, topk, F32)
        work = list(sc)
        for r in range(PEER_TOPK):
            for p in range(2):
                m = jnp.max(work[p], axis=0, keepdims=True)
                hit = work[p] == m
                if p == 1:
                    rank2 = jnp.where(hit, float(r), rank2)
                work[p] = jnp.where(hit, NEG_INF, work[p])
                ts_scr[p, r:r + 1, :] = m
        for p in range(2):
            taken = jnp.sum(jnp.where(work[p] == NEG_INF, 1.0, 0.0), axis=0, keepdims=True)
            bad = jnp.maximum(bad, jnp.where(taken != topk, 1.0, 0.0))
        ts1, ts2 = ts_scr[0], ts_scr[1]
        cand = (_select_rows(sel1_ref[...], ts1) + _select_rows(sel2_ref[...], ts2)) + pad_ref[...]
        m0 = ts1[0:1] + ts2[0:1]
        zsum = jnp.zeros_like(m0)
        for r in range(PEER_TOPK):
            m = jnp.max(cand, axis=0, keepdims=True)
            cand = jnp.where(cand == m, NEG_INF, cand)
            zsum = zsum + jnp.exp(m - m0)
        gone = jnp.where(cand == NEG_INF, 1.0, 0.0)
        bad = jnp.maximum(bad, jnp.where(jnp.sum(gone, axis=0, keepdims=True) != topk + _CAND_PAD, 1.0, 0.0))
        per_rank = _mm(cnt_ref[...], gone.astype(BF16))
        n1 = jnp.zeros(sc[0].shape, F32)
        for a in range(PEER_TOPK):
            n1 = jnp.where(sc[0] == ts1[a:a + 1], per_rank[a:a + 1], n1)
        in1 = sc[0] >= ts1[PEER_TOPK - 1:PEER_TOPK]
        in2 = sc[1] >= ts2[PEER_TOPK - 1:PEER_TOPK]
        a_ref[h] = jnp.where(in1, jnp.exp(jnp.where(in1, sc[0] - ts1[0:1], 0.0)), 0.0)
        n1_ref[h] = n1
        b_ref[h] = jnp.where(in2, jnp.exp(jnp.where(in2, sc[1] - ts2[0:1], 0.0)) / zsum, 0.0).astype(ROUTE_DTYPE)
        r2_ref[h] = rank2.astype(ROUTE_DTYPE)
    return bad


def _route_kernel(q_ref, keys_ref, sel1_ref, sel2_ref, cnt_ref, pad_ref, a_ref, n1_ref, b_ref, r2_ref, ts_scr):
    outs = (a_ref, n1_ref, b_ref, r2_ref)
    bad = _route_distinct(q_ref, keys_ref, sel1_ref, sel2_ref, cnt_ref, pad_ref, *outs, ts_scr)

    @pl.when(jnp.max(bad) > 0.0)
    def _():
        _route_exact(q_ref, keys_ref, *outs, ts_scr)


def _route(q, keys, tb):
    t = q.shape[0]
    sel1 = np.zeros((_CAND_N, PEER_TOPK), np.float32)
    sel2 = np.zeros((_CAND_N, PEER_TOPK), np.float32)
    pad = np.full((_CAND_N, 1), NEG_INF, np.float32)
    for row, (a, b) in enumerate(_CAND_PAIRS):
        sel1[row, a] = 1.0
        sel2[row, b] = 1.0
        pad[row, 0] = 0.0
    consts = (jnp.asarray(sel1, BF16), jnp.asarray(sel2, BF16), jnp.asarray(sel1.T, BF16), jnp.asarray(pad))
    row_shp = jax.ShapeDtypeStruct((PEER_HEADS, PEER_N_KEYS, t), F32)
    key_shp = jax.ShapeDtypeStruct((PEER_HEADS, PEER_N_KEYS, t), ROUTE_DTYPE)
    ospec = pl.BlockSpec((PEER_HEADS, PEER_N_KEYS, tb), lambda i: (0, 0, i))
    return pl.pallas_call(
        _route_kernel,
        out_shape=(row_shp, row_shp, key_shp, key_shp),
        grid=(t // tb,),
        in_specs=[pl.BlockSpec((tb, q.shape[1]), lambda i: (i, 0)),
                  pl.BlockSpec(keys.shape, lambda i: (0, 0, 0))]
                 + [pl.BlockSpec(c.shape, lambda i: (0, 0)) for c in consts],
        out_specs=(ospec, ospec, ospec, ospec),
        scratch_shapes=[pltpu.VMEM((2, PEER_TOPK, tb), F32)],
        compiler_params=_params("arbitrary"),
        name="route",
    )(q, keys, *consts)


EXPERT_ROWS = 16
EXPERT_TB = 512
BF16_ROWS = 16


def _experts_kernel(h2_ref, u_ref, vt_ref, a_ref, n1_ref, b_ref, r2_ref, x1_ref, mod_ref, o_ref, acc, hid, act):
    e = pl.program_id(1)

    @pl.when(e == 0)
    def _():
        acc[...] = jnp.zeros_like(acc)

    nk = PEER_N_KEYS
    tb = hid.shape[1]
    hid[...] = _mm_nt(u_ref[...], h2_ref[...])
    for ii in range(EXPERT_ROWS):
        gate = None
        for h in range(PEER_HEADS):
            count = jnp.broadcast_to(n1_ref[h, ii:ii + 1, :], (BF16_ROWS, tb)).astype(BF16)[None]
            scale = jnp.broadcast_to(a_ref[h, ii:ii + 1, :], (BF16_ROWS, tb)).astype(BF16)[None]
            term = jnp.where(r2_ref[h] < count, b_ref[h], jnp.zeros((), BF16)) * scale
            gate = term if gate is None else gate + term
        rows = slice(ii * nk, (ii + 1) * nk)
        act[rows, :] = _gelu(hid[rows, :]).astype(BF16) * gate.reshape(nk, tb)
    acc[...] += _mm(vt_ref[...], act[...])

    @pl.when(e == pl.num_programs(1) - 1)
    def _():
        o_ref[...] = x1_ref[...] + mod_ref[5:6, :] * acc[...].T


def _experts(h2, u_bf, vt_bf, a, n1, b, r2, x1, mod3, seq, tb):
    t, d = h2.shape
    per_seq = seq // tb
    ne = u_bf.shape[0]
    eb = EXPERT_ROWS * PEER_N_KEYS
    tiles = PEER_N_KEYS // BF16_ROWS
    b = b.reshape(PEER_HEADS, tiles, BF16_ROWS, t)
    r2 = r2.reshape(PEER_HEADS, tiles, BF16_ROWS, t)
    rows = pl.BlockSpec((PEER_HEADS, EXPERT_ROWS, tb), lambda i, e: (0, e, i))
    allk = pl.BlockSpec((PEER_HEADS, tiles, BF16_ROWS, tb), lambda i, e: (0, 0, 0, i))
    return pl.pallas_call(
        _experts_kernel,
        out_shape=jax.ShapeDtypeStruct((t, d), F32),
        grid=(t // tb, ne // eb),
        in_specs=[pl.BlockSpec((tb, d), lambda i, e: (i, 0)),
                  pl.BlockSpec((eb, d), lambda i, e: (e, 0)),
                  pl.BlockSpec((d, eb), lambda i, e: (0, e)),
                  rows, rows, allk, allk,
                  pl.BlockSpec((tb, d), lambda i, e: (i, 0)),
                  pl.BlockSpec((None, 6, d), lambda i, e: (i // per_seq, 0, 0))],
        out_specs=pl.BlockSpec((tb, d), lambda i, e: (i, 0)),
        scratch_shapes=[pltpu.VMEM((d, tb), F32), pltpu.VMEM((eb, tb), F32), pltpu.VMEM((eb, tb), BF16)],
        compiler_params=_params("arbitrary", "arbitrary"),
        name="experts",
    )(h2, u_bf, vt_bf, a, n1, b, r2, x1, mod3)


def _regroup_w_in(w):
    d = D_MODEL
    o = 0
    parts = {}
    for name, width in (("aq", ATTN_Q_WIDTH), ("ak", ATTN_KV_WIDTH), ("av", ATTN_KV_WIDTH), ("dq", DN_WIDTH),
                        ("dk", DN_WIDTH), ("dv", DN_WIDTH), ("dz", DN_WIDTH), ("ba", 2 * DN_HEADS),
                        ("ga", d), ("gb", d)):
        parts[name] = w[:, o:o + width]
        o += width
    big = jnp.concatenate([parts[n] for n in ("aq", "dq", "dk", "dv", "dz", "ga", "gb", "ak", "av")], axis=1)
    return big.astype(BF16), parts["ba"], parts["ba"].T


def _layer(x2, c, seq, w_ada, b_ada, norm1_gain, w_in, attn_q_norm, attn_k_norm, attn_sinks, dn_conv_w, dn_a_log,
           dn_dt_bias, dn_out_norm, w_attn_branch, w_dn_branch, w_out, norm2_gain, peer_w_q, peer_sub_keys,
           peer_u, peer_v):
    d = D_MODEL
    row = lambda a: a.reshape(1, -1)
    tb = min(512, seq)

    mod3 = _adaln(c, w_ada, b_ada).reshape(c.shape[0], 6, d)
    w_big, w_ba, w_bat = _regroup_w_in(w_in)
    proj, ba, bat = _proj(x2, mod3, row(norm1_gain), w_big, w_ba, w_bat, seq, tb)
    attn = _attention(proj, row(attn_q_norm), row(attn_k_norm), row(attn_sinks), seq)
    od = _deltanet(proj, ba, bat, dn_conv_w, dn_a_log, dn_dt_bias, dn_out_norm, seq)
    wq_hi = peer_w_q.astype(BF16)
    wq_lo = (peer_w_q - wq_hi.astype(F32)).astype(BF16)
    x1, h2, q = _mix(attn, od, proj, x2, mod3, w_attn_branch.astype(BF16), w_dn_branch.astype(BF16),
                     w_out.astype(BF16), row(norm2_gain), wq_hi, wq_lo, seq, tb)
    keys = peer_sub_keys.reshape(2 * PEER_HEADS, PEER_N_KEYS, PEER_HALF)
    a, n1, b, r2 = _route(q, keys, min(ROUTE_TB, seq))
    return _experts(h2, peer_u.astype(BF16), peer_v.T.astype(BF16), a, n1, b, r2, x1, mod3, seq, min(EXPERT_TB, seq))


def kernel(x, c, w_ada, b_ada, norm1_gain, w_in, attn_q_norm, attn_k_norm, attn_sinks, dn_conv_w, dn_a_log, dn_dt_bias, dn_out_norm, w_attn_branch, w_dn_branch, w_out, norm2_gain, peer_w_q, peer_sub_keys, peer_u, peer_v):
    b, s, d = x.shape
    x2 = x.reshape(b * s, d)
    for l in range(w_ada.shape[0]):
        x2 = _layer(x2, c, s, w_ada[l], b_ada[l], norm1_gain[l], w_in[l], attn_q_norm[l], attn_k_norm[l],
                    attn_sinks[l], dn_conv_w[l], dn_a_log[l], dn_dt_bias[l], dn_out_norm[l], w_attn_branch[l],
                    w_dn_branch[l], w_out[l], norm2_gain[l], peer_w_q[l], peer_sub_keys[l], peer_u[l], peer_v[l])
    return x2.reshape(b, s, d)
```

```python
import functools

import jax
import jax.numpy as jnp
import numpy as np
from jax import lax
from jax.experimental import pallas as pl
from jax.experimental.pallas import tpu as pltpu

F32 = jnp.float32
BF16 = jnp.bfloat16

D_MODEL = 1024
ATTN_HEADS = 16
ATTN_KV_HEADS = 4
ATTN_HEAD_DIM = 64
ATTN_GROUP = ATTN_HEADS // ATTN_KV_HEADS
ATTN_Q_WIDTH = ATTN_HEADS * ATTN_HEAD_DIM
ATTN_KV_WIDTH = ATTN_KV_HEADS * ATTN_HEAD_DIM
WINDOW = 128
DN_HEADS = 8
DN_DIM = 128
DN_WIDTH = DN_HEADS * DN_DIM
DN_CONV = 4
DN_CHUNK = 64
PEER_HEADS = 8
PEER_N_KEYS = 128
PEER_HALF = 128
PEER_TOPK = 16
EPS = 1e-6
NEG = -1e30
NEG_INF = float("-inf")

VMEM_LIMIT = 48 * 1024 * 1024


def _params(*sem):
    return pltpu.CompilerParams(dimension_semantics=sem, vmem_limit_bytes=VMEM_LIMIT)


def _mm(a, b):
    return jnp.dot(a, b, preferred_element_type=F32)


def _mm_nt(a, b):
    return lax.dot_general(a, b, (((1,), (1,)), ((), ())), preferred_element_type=F32)


def _bmm(a, b):
    return lax.dot_general(a, b, (((2,), (1,)), ((0,), (0,))), preferred_element_type=F32)


def _bmm_nt(a, b):
    return lax.dot_general(a, b, (((2,), (2,)), ((0,), (0,))), preferred_element_type=F32)


def _split(a):
    hi = a.astype(BF16)
    lo = (a - hi.astype(F32)).astype(BF16)
    return hi, lo


def _three_pass(mm, a, b):
    ah, al = _split(a)
    bh, bl = _split(b)
    return mm(ah, bh) + (mm(al, bh) + mm(ah, bl))


def _silu(v):
    return v * jax.nn.sigmoid(v)


def _gelu(v):
    return 0.5 * v * (1.0 + lax.erf(v * (2.0 ** -0.5)))


def _adaln_kernel(c_ref, w_ref, b_ref, o_ref):
    o_ref[...] = _three_pass(_mm, _silu(c_ref[...]), w_ref[...]) + b_ref[...]


def _adaln(c, w_ada, b_ada):
    b, d = c.shape
    n = w_ada.shape[1] // d
    return pl.pallas_call(
        _adaln_kernel,
        out_shape=jax.ShapeDtypeStruct((b, n * d), F32),
        grid=(n,),
        in_specs=[pl.BlockSpec((b, d), lambda j: (0, 0)),
                  pl.BlockSpec((d, d), lambda j: (0, j)),
                  pl.BlockSpec((1, d), lambda j: (0, j))],
        out_specs=pl.BlockSpec((b, d), lambda j: (0, j)),
        compiler_params=_params("arbitrary"),
        name="adaln",
    )(c, w_ada, b_ada.reshape(1, -1))


PROJ_COLS = 1536
COL_AQ, COL_DQ, COL_DK, COL_DV, COL_DZ, COL_GA, COL_GB = range(7)
COL_AK = 7 * D_MODEL // ATTN_KV_WIDTH
COL_AV = COL_AK + 1
PROJ_WIDTH = 7 * D_MODEL + 2 * ATTN_KV_WIDTH


def _proj_kernel(x_ref, mod_ref, g_ref, w_ref, wba_ref, o_ref, ba_ref, bat_ref, h_scr):
    @pl.when(pl.program_id(1) == 0)
    def _():
        x = x_ref[...]
        y = x * lax.rsqrt(jnp.mean(x * x, axis=-1, keepdims=True) + EPS) * g_ref[...]
        mod = mod_ref[...]
        h = y * (1.0 + mod[1:2]) + mod[0:1]
        h_scr[...] = h.astype(BF16)
        ba = _three_pass(_mm, h, wba_ref[...])
        ba_ref[...] = ba
        n = ba.shape[1]
        eye = jnp.where(lax.broadcasted_iota(jnp.int32, (n, n), 0) == lax.broadcasted_iota(jnp.int32, (n, n), 1),
                        1.0, 0.0).astype(BF16)
        p0 = ba.astype(BF16)
        r1 = ba - p0.astype(F32)
        p1 = r1.astype(BF16)
        p2 = (r1 - p1.astype(F32)).astype(BF16)
        bat_ref[...] = _mm_nt(eye, p0) + (_mm_nt(eye, p1) + _mm_nt(eye, p2))

    o_ref[...] = _mm(h_scr[...], w_ref[...]).astype(BF16)


def _proj(x2, mod3, gain, w_big, w_ba, seq, tb):
    t, d = x2.shape
    per_seq = seq // tb
    nba = w_ba.shape[1]
    return pl.pallas_call(
        _proj_kernel,
        out_shape=(jax.ShapeDtypeStruct((t, PROJ_WIDTH), BF16),
                   jax.ShapeDtypeStruct((t, nba), F32),
                   jax.ShapeDtypeStruct((nba, t), F32)),
        grid=(t // tb, PROJ_WIDTH // PROJ_COLS),
        in_specs=[pl.BlockSpec((tb, d), lambda i, j: (i, 0)),
                  pl.BlockSpec((None, 6, d), lambda i, j: (i // per_seq, 0, 0)),
                  pl.BlockSpec((1, d), lambda i, j: (0, 0)),
                  pl.BlockSpec((d, PROJ_COLS), lambda i, j: (0, j)),
                  pl.BlockSpec((d, nba), lambda i, j: (0, 0))],
        out_specs=(pl.BlockSpec((tb, PROJ_COLS), lambda i, j: (i, j)),
                   pl.BlockSpec((tb, nba), lambda i, j: (i, 0)),
                   pl.BlockSpec((nba, tb), lambda i, j: (0, i))),
        scratch_shapes=[pltpu.VMEM((tb, d), BF16)],
        compiler_params=_params("arbitrary", "arbitrary"),
        name="proj",
    )(x2, mod3, gain, w_big, w_ba)


ATTN_TQ = 256
LANES = 128


def _head_scale(x, seg, segt):
    hi, lo = _split(x * x)
    ss = _mm(hi, seg) + _mm(lo, seg)
    rh, rl = _split(lax.rsqrt(ss * (1.0 / ATTN_HEAD_DIM) + EPS))
    return _mm(rh, segt) + _mm(rl, segt)


def _attn_kernel(q_ref, kc_ref, vc_ref, kp_ref, vp_ref, qg_ref, kg_ref, sink_ref, seg_ref, segt_ref, o_ref,
                 *, steps_per_seq):
    blk = WINDOW
    tq = q_ref.shape[0]
    first = (pl.program_id(0) % steps_per_seq) == 0
    seg, segt = seg_ref[...], segt_ref[...]
    kvw = ATTN_KV_WIDTH

    q = q_ref[...].astype(F32)
    qn = (q * _head_scale(q, seg, segt) * (qg_ref[...] * (ATTN_HEAD_DIM ** -0.5))).astype(BF16)
    k = jnp.concatenate([kp_ref[...], kc_ref[...]], axis=0).astype(F32)
    kn = k * _head_scale(k, seg[:kvw], segt[:, :kvw]) * kg_ref[...]
    v = jnp.concatenate([vp_ref[...], vc_ref[...]], axis=0).astype(F32)

    left = lax.broadcasted_iota(jnp.int32, (blk + tq, LANES), 1) < ATTN_HEAD_DIM

    def halves(x):
        out = []
        for tile in range(kvw // LANES):
            xt = x[:, tile * LANES:(tile + 1) * LANES]
            xr = pltpu.roll(xt, ATTN_HEAD_DIM, 1)
            out.append((jnp.where(left, xt, 0.0).astype(BF16), jnp.where(left, 0.0, xr).astype(BF16)))
            out.append((jnp.where(left, xr, 0.0).astype(BF16), jnp.where(left, 0.0, xt).astype(BF16)))
        return out

    k_parts, v_parts = halves(kn), halves(v)

    keys = 2 * blk
    qi = lax.broadcasted_iota(jnp.int32, (blk, 2 * keys), 0)
    kj = lax.broadcasted_iota(jnp.int32, (blk, 2 * keys), 1) & (keys - 1)
    rel = qi + blk - kj
    band = (rel >= 0) & (rel < WINDOW)
    col = lax.broadcasted_iota(jnp.int32, (1, 2 * keys), 1)
    sinks = sink_ref[...]
    row = lax.broadcasted_iota(jnp.int32, (2 * keys, LANES), 0)
    lane = lax.broadcasted_iota(jnp.int32, (2 * keys, LANES), 1)
    sink_row = (row & (keys - 1)) == 0
    ones_blk = jnp.where((row < keys) == (lane < ATTN_HEAD_DIM), 1.0, 0.0).astype(BF16)
    pairs = ATTN_HEADS // 2
    per_kv = ATTN_GROUP // 2

    for sb in range(tq // blk):
        krows = slice(sb * blk, sb * blk + keys)
        qrows = slice(sb * blk, (sb + 1) * blk)
        mask = band & (jnp.logical_not(first) | (kj >= blk)) if sb == 0 else band
        kblk = [jnp.concatenate([a[krows], b[krows]], axis=0) for a, b in k_parts]
        vblk = [jnp.where(sink_row, jnp.zeros((), BF16), jnp.concatenate([a[krows], b[krows]], axis=0))
                for a, b in v_parts]
        scores = [_mm_nt(qn[qrows, j * LANES:(j + 1) * LANES], kblk[j // per_kv]) for j in range(pairs)]
        fills = [jnp.where(col == 0, sinks[0:1, 2 * j:2 * j + 1],
                           jnp.where(col == keys, sinks[0:1, 2 * j + 1:2 * j + 2], NEG)) for j in range(pairs)]
        logits = [jnp.where(mask, scores[j], fills[j]) for j in range(pairs)]
        tops = [[jnp.max(s[:, u * keys:(u + 1) * keys], axis=-1, keepdims=True) for u in range(2)] for s in logits]
        probs = [jnp.exp(s - jnp.concatenate([jnp.broadcast_to(m, (blk, keys)) for m in ms], axis=1)).astype(BF16)
                 for s, ms in zip(logits, tops)]
        outs = [_mm(p, vblk[j // per_kv]) for j, p in enumerate(probs)]
        dens = [_mm(p, ones_blk) for p in probs]
        for j in range(pairs):
            o_ref[qrows, j * LANES:(j + 1) * LANES] = (outs[j] / dens[j]).astype(BF16)


def _attention(proj, qn, kn, sinks, seq):
    t = proj.shape[0]
    tq = min(ATTN_TQ, seq)
    per_seq = seq // tq
    halo = tq // WINDOW
    prev = lambda i: jnp.maximum(i * halo - 1, 0)
    head_of_lane = np.arange(ATTN_Q_WIDTH) // ATTN_HEAD_DIM
    seg = (head_of_lane[:, None] == np.arange(ATTN_HEADS)[None, :]).astype(np.float32)
    consts = (jnp.tile(qn, (1, ATTN_HEADS)), jnp.tile(kn, (1, ATTN_KV_HEADS)), sinks,
              jnp.asarray(seg, BF16), jnp.asarray(seg.T, BF16))
    return pl.pallas_call(
        functools.partial(_attn_kernel, steps_per_seq=per_seq),
        out_shape=jax.ShapeDtypeStruct((t, ATTN_Q_WIDTH), BF16),
        grid=(t // tq,),
        in_specs=[pl.BlockSpec((tq, ATTN_Q_WIDTH), lambda i: (i, COL_AQ)),
                  pl.BlockSpec((tq, ATTN_KV_WIDTH), lambda i: (i, COL_AK)),
                  pl.BlockSpec((tq, ATTN_KV_WIDTH), lambda i: (i, COL_AV)),
                  pl.BlockSpec((WINDOW, ATTN_KV_WIDTH), lambda i: (prev(i), COL_AK)),
                  pl.BlockSpec((WINDOW, ATTN_KV_WIDTH), lambda i: (prev(i), COL_AV))]
                 + [pl.BlockSpec(c.shape, lambda i: (0, 0)) for c in consts],
        out_specs=pl.BlockSpec((tq, ATTN_Q_WIDTH), lambda i: (i, 0)),
        compiler_params=_params("arbitrary"),
        name="attn",
    )(proj, proj, proj, proj, proj, *consts)


DN_BLOCK = 128
CONV_HALO = 8


def _dn_kernel(q_ref, k_ref, v_ref, pq_ref, pk_ref, pv_ref, z_ref, ba_ref, bat_ref, cw_ref,
               alog_r, dt_r, alog_c, dt_c, on_ref, o_ref, state, *, blocks_per_seq):
    first = (pl.program_id(0) % blocks_per_seq) == 0
    cb, cl, nh = DN_BLOCK, DN_CHUNK, DN_HEADS

    @pl.when(first)
    def _():
        state[...] = jnp.zeros_like(state)

    def conv(cur_ref, prev_ref, part):
        cur = cur_ref[...].astype(F32)
        prev = jnp.where(first, 0.0, prev_ref[...].astype(F32))
        xc = jnp.concatenate([prev, cur], axis=0)
        acc = None
        for j in range(DN_CONV):
            w = cw_ref[j:j + 1, part * DN_WIDTH:(part + 1) * DN_WIDTH]
            shift = DN_CONV - 1 - j
            xs = pltpu.roll(xc, shift, 0) if shift else xc
            term = w * xs[CONV_HALO:CONV_HALO + cb]
            acc = term if acc is None else acc + term
        return _silu(acc)

    qc, kc, vc = conv(q_ref, pq_ref, 0), conv(k_ref, pk_ref, 1), conv(v_ref, pv_ref, 2)

    ba = ba_ref[...]
    bat = bat_ref[...]
    beta = jax.nn.sigmoid(ba[:, :nh])
    g_col = -jnp.exp(alog_r[...]) * jax.nn.softplus(ba[:, nh:] + dt_r[...])
    g_row = -jnp.exp(alog_c[...]) * jax.nn.softplus(bat[nh:] + dt_c[...])

    r = lax.broadcasted_iota(jnp.int32, (cb, cb), 0)
    c = lax.broadcasted_iota(jnp.int32, (cb, cb), 1)
    same = (r // cl) == (c // cl)
    lower = jnp.where(same & (c <= r), 1.0, 0.0).astype(BF16)
    upper = jnp.where(same & (r <= c), 1.0, 0.0).astype(BF16)

    def exact3(mm, a, b, split_left):
        v0 = a if split_left else b
        p0 = v0.astype(BF16)
        r1 = v0 - p0.astype(F32)
        p1 = r1.astype(BF16)
        p2 = (r1 - p1.astype(F32)).astype(BF16)
        if split_left:
            return mm(p0, b) + (mm(p1, b) + mm(p2, b))
        return mm(a, p0) + (mm(a, p1) + mm(a, p2))

    gc_col = exact3(_mm, lower, g_col, False)
    gc_row = exact3(_mm, g_row, upper, True)

    ri = lax.broadcasted_iota(jnp.int32, (cl, cl), 0)
    ci = lax.broadcasted_iota(jnp.int32, (cl, cl), 1)
    tril = ci <= ri
    strict = ci < ri
    eye = jnp.where(ci == ri, 1.0, 0.0)
    eye_k = jnp.where(lax.broadcasted_iota(jnp.int32, (DN_DIM, DN_DIM), 0)
                      == lax.broadcasted_iota(jnp.int32, (DN_DIM, DN_DIM), 1), 1.0, 0.0).astype(BF16)

    z = z_ref[...].astype(F32)
    for ch in range(cb // cl):
        t0 = ch * cl
        qs, ks, kbs, kgs, vbs, decs, qgs, kds, gls = [], [], [], [], [], [], [], [], []
        for h in range(nh):
            sl = slice(h * DN_DIM, (h + 1) * DN_DIM)
            qh = qc[t0:t0 + cl, sl]
            kh = kc[t0:t0 + cl, sl]
            vh = vc[t0:t0 + cl, sl]
            qh = qh * lax.rsqrt(jnp.sum(qh * qh, axis=-1, keepdims=True) + EPS) * (DN_DIM ** -0.5)
            kh = kh * lax.rsqrt(jnp.sum(kh * kh, axis=-1, keepdims=True) + EPS)
            bh = beta[t0:t0 + cl, h:h + 1]
            gcol = gc_col[t0:t0 + cl, h:h + 1]
            grow = gc_row[h:h + 1, t0:t0 + cl]
            glast = gc_col[t0 + cl - 1:t0 + cl, h:h + 1]
            dec = jnp.where(tril, jnp.exp(jnp.where(tril, gcol - grow, 0.0)), 0.0)
            eg = jnp.exp(gcol)
            kb = kh * bh
            qs.append(qh); ks.append(kh); kbs.append(kb); kgs.append(kb * eg)
            vbs.append(vh * bh); decs.append(dec); qgs.append(qh * eg)
            kds.append(kh * jnp.exp(glast - gcol)); gls.append(jnp.exp(glast))
        q3, k3 = jnp.stack(qs).astype(BF16), jnp.stack(ks).astype(BF16)
        kb3, kg3, vb3 = jnp.stack(kbs).astype(BF16), jnp.stack(kgs).astype(BF16), jnp.stack(vbs).astype(BF16)
        dec3, qg3, kd3 = jnp.stack(decs), jnp.stack(qgs).astype(BF16), jnp.stack(kds).astype(BF16)

        kq = _bmm_nt(jnp.concatenate([kb3, q3], axis=1), k3)
        a_low = jnp.where(strict[None], kq[:, :cl] * dec3, 0.0)
        qk3 = kq[:, cl:] * dec3
        pw = -a_low
        t_mat = eye[None] + pw
        pw = _three_pass(_bmm, pw, pw)
        for m in range(1, 6):
            both = jnp.concatenate([pw, t_mat], axis=1)
            prod = _three_pass(_bmm, both, pw) if m == 1 else _bmm(both.astype(BF16), pw.astype(BF16))
            pw = prod[:, :cl]
            t_mat = t_mat + prod[:, cl:]
        uw = _bmm(t_mat.astype(BF16), jnp.concatenate([vb3, kg3], axis=2))
        u3, w3 = uw[:, :, :DN_DIM], uw[:, :, DN_DIM:]

        s_old = state[...]
        s_bf = s_old.astype(BF16)
        ws_qs = _bmm(jnp.concatenate([w3.astype(BF16), qg3], axis=1), s_bf)
        v_new = u3 - ws_qs[:, :cl]
        o3 = ws_qs[:, cl:] + _bmm(qk3.astype(BF16), v_new.astype(BF16))
        kdt = _bmm_nt(jnp.broadcast_to(eye_k[None], (nh, DN_DIM, DN_DIM)), kd3)
        upd = _bmm(kdt.astype(BF16), v_new.astype(BF16))
        state[...] = jnp.stack([s_old[h] * gls[h] for h in range(nh)]) + upd

        for h in range(nh):
            sl = slice(h * DN_DIM, (h + 1) * DN_DIM)
            oh = o3[h]
            oh = oh * lax.rsqrt(jnp.mean(oh * oh, axis=-1, keepdims=True) + EPS) * on_ref[...]
            o_ref[t0:t0 + cl, sl] = (oh * _silu(z[t0:t0 + cl, sl])).astype(BF16)


def _deltanet(proj, ba, bat, conv_w, a_log, dt_bias, out_norm, seq):
    t = proj.shape[0]
    cb = DN_BLOCK
    per_seq = seq // cb
    halo = cb // CONV_HALO
    prev = lambda i: jnp.maximum(i * halo - 1, 0)
    nba = ba.shape[1]
    row = lambda a: a.reshape(1, -1)
    col = lambda a: a.reshape(-1, 1)
    small = lambda shp: pl.BlockSpec(shp, lambda i: (0, 0))
    return pl.pallas_call(
        functools.partial(_dn_kernel, blocks_per_seq=per_seq),
        out_shape=jax.ShapeDtypeStruct((t, DN_WIDTH), BF16),
        grid=(t // cb,),
        in_specs=[pl.BlockSpec((cb, DN_WIDTH), lambda i: (i, COL_DQ)),
                  pl.BlockSpec((cb, DN_WIDTH), lambda i: (i, COL_DK)),
                  pl.BlockSpec((cb, DN_WIDTH), lambda i: (i, COL_DV)),
                  pl.BlockSpec((CONV_HALO, DN_WIDTH), lambda i: (prev(i), COL_DQ)),
                  pl.BlockSpec((CONV_HALO, DN_WIDTH), lambda i: (prev(i), COL_DK)),
                  pl.BlockSpec((CONV_HALO, DN_WIDTH), lambda i: (prev(i), COL_DV)),
                  pl.BlockSpec((cb, DN_WIDTH), lambda i: (i, COL_DZ)),
                  pl.BlockSpec((cb, nba), lambda i: (i, 0)),
                  pl.BlockSpec((nba, cb), lambda i: (0, i)),
                  small((DN_CONV, 3 * DN_WIDTH)),
                  small((1, DN_HEADS)), small((1, DN_HEADS)),
                  small((DN_HEADS, 1)), small((DN_HEADS, 1)),
                  small((1, DN_DIM))],
        out_specs=pl.BlockSpec((cb, DN_WIDTH), lambda i: (i, 0)),
        scratch_shapes=[pltpu.VMEM((DN_HEADS, DN_DIM, DN_DIM), F32)],
        compiler_params=_params("arbitrary"),
        name="deltanet",
    )(proj, proj, proj, proj, proj, proj, proj, ba, bat, conv_w,
      row(a_log), row(dt_bias), col(a_log), col(dt_bias), row(out_norm))


def _mix_kernel(att_ref, od_ref, ga_ref, gb_ref, x_ref, mod_ref, wa_ref, wb_ref, wo_ref, g2_ref,
                wq_ref, x1_ref, h2_ref, q_ref):
    ya = _mm(att_ref[...], wa_ref[...])
    yb = _mm(od_ref[...], wb_ref[...])
    mixed = jax.nn.sigmoid(ga_ref[...].astype(F32)) * ya + jax.nn.sigmoid(gb_ref[...].astype(F32)) * yb
    mod = mod_ref[...]
    x1 = x_ref[...] + mod[2:3] * _mm(mixed.astype(BF16), wo_ref[...])
    x1_ref[...] = x1
    y = x1 * lax.rsqrt(jnp.mean(x1 * x1, axis=-1, keepdims=True) + EPS) * g2_ref[...]
    h2 = y * (1.0 + mod[4:5]) + mod[3:4]
    h2_bf = h2.astype(BF16)
    h2_ref[...] = h2_bf
    q_ref[...] = _mm(h2_bf, wq_ref[...])


def _mix(attn, od, proj, x2, mod3, wa, wb, wo, gain2, wq, seq, tb):
    t, d = x2.shape
    per_seq = seq // tb
    nq = wq.shape[1]
    full = lambda shp: pl.BlockSpec(shp, lambda i: (0, 0))
    return pl.pallas_call(
        _mix_kernel,
        out_shape=(jax.ShapeDtypeStruct((t, d), F32),
                   jax.ShapeDtypeStruct((t, d), BF16),
                   jax.ShapeDtypeStruct((t, nq), F32)),
        grid=(t // tb,),
        in_specs=[pl.BlockSpec((tb, d), lambda i: (i, 0)),
                  pl.BlockSpec((tb, d), lambda i: (i, 0)),
                  pl.BlockSpec((tb, d), lambda i: (i, COL_GA)),
                  pl.BlockSpec((tb, d), lambda i: (i, COL_GB)),
                  pl.BlockSpec((tb, d), lambda i: (i, 0)),
                  pl.BlockSpec((None, 6, d), lambda i: (i // per_seq, 0, 0)),
                  full((d, d)), full((d, d)), full((d, d)), full((1, d)),
                  full((d, nq))],
        out_specs=(pl.BlockSpec((tb, d), lambda i: (i, 0)),
                   pl.BlockSpec((tb, d), lambda i: (i, 0)),
                   pl.BlockSpec((tb, nq), lambda i: (i, 0))),
        compiler_params=_params("arbitrary"),
        name="mix",
    )(attn, od, proj, proj, x2, mod3, wa, wb, wo, gain2, wq)


ROUTE_DTYPE = BF16
ROUTE_TB = 256

_CAND_ROWS = tuple(PEER_TOPK // (a + 1) for a in range(PEER_TOPK))


def _scores(q_ref, keys_ref, hp):
    return _three_pass(_mm_nt, keys_ref[hp], q_ref[:, hp * PEER_HALF:(hp + 1) * PEER_HALF])


_CAND_PAIRS = tuple((a, b) for a in range(PEER_TOPK) for b in range(_CAND_ROWS[a]))
_CAND_PAD = -len(_CAND_PAIRS) % 8
_CAND_N = len(_CAND_PAIRS) + _CAND_PAD


def _select_rows(onehot, vals):
    p0 = vals.astype(BF16)
    r1 = vals - p0.astype(F32)
    p1 = r1.astype(BF16)
    p2 = (r1 - p1.astype(F32)).astype(BF16)
    return _mm(onehot, p0) + (_mm(onehot, p1) + _mm(onehot, p2))


def _route_pass(q_ref, keys_ref, sel1_ref, sel2_ref, cnt_ref, pad_ref, a_ref, n1_ref, b_ref, r2_ref, ts_scr,
                *, index_order):
    tb = q_ref.shape[0]
    topk = float(PEER_TOPK)
    bad = jnp.zeros((1, tb), F32)
    cand_row = lax.broadcasted_iota(jnp.int32, (_CAND_N, tb), 0)
    key_row = lax.broadcasted_iota(jnp.int32, (PEER_N_KEYS, tb), 0)
    for h in range(PEER_HEADS):
        sc = [_scores(q_ref, keys_ref, 2 * h + p) for p in range(2)]
        rank = [jnp.full(s.shape, topk, F32) for s in sc]
        work = list(sc)
        for r in range(PEER_TOPK):
            for p in range(2):
                m = jnp.max(work[p], axis=0, keepdims=True)
                hit = work[p] == m
                if index_order:
                    hit = key_row == jnp.min(jnp.where(hit, key_row, PEER_N_KEYS), axis=0, keepdims=True)
                if index_order or p == 1:
                    rank[p] = jnp.where(hit, float(r), rank[p])
                work[p] = jnp.where(hit, NEG_INF, work[p])
                ts_scr[p, r:r + 1, :] = m
        if not index_order:
            for p in range(2):
                taken = jnp.sum(jnp.where(work[p] == NEG_INF, 1.0, 0.0), axis=0, keepdims=True)
                bad = jnp.maximum(bad, jnp.where(taken != topk, 1.0, 0.0))
        ts1, ts2 = ts_scr[0], ts_scr[1]
        cand = (_select_rows(sel1_ref[...], ts1) + _select_rows(sel2_ref[...], ts2)) + pad_ref[...]
        m0 = ts1[0:1] + ts2[0:1]
        zsum = jnp.zeros_like(m0)
        for r in range(PEER_TOPK):
            m = jnp.max(cand, axis=0, keepdims=True)
            first = jnp.min(jnp.where(cand == m, cand_row, _CAND_N), axis=0, keepdims=True)
            cand = jnp.where(cand_row == first, NEG_INF, cand)
            zsum = zsum + jnp.exp(m - m0)
        gone = jnp.where(cand == NEG_INF, 1.0, 0.0)
        per_rank = _mm(cnt_ref[...], gone.astype(BF16))
        n1 = jnp.zeros(sc[0].shape, F32)
        for a in range(PEER_TOPK):
            is_rank_a = rank[0] == float(a) if index_order else sc[0] == ts1[a:a + 1]
            n1 = jnp.where(is_rank_a, per_rank[a:a + 1], n1)
        in1 = rank[0] < topk if index_order else sc[0] >= ts1[PEER_TOPK - 1:PEER_TOPK]
        in2 = rank[1] < topk
        a_ref[h] = jnp.where(in1, jnp.exp(jnp.where(in1, sc[0] - ts1[0:1], 0.0)), 0.0)
        n1_ref[h] = n1
        b_ref[h] = jnp.where(in2, jnp.exp(jnp.where(in2, sc[1] - ts2[0:1], 0.0)) / zsum, 0.0).astype(ROUTE_DTYPE)
        r2_ref[h] = rank[1].astype(ROUTE_DTYPE)
    return bad


def _route_kernel(*refs):
    bad = _route_pass(*refs, index_order=False)

    @pl.when(jnp.max(bad) > 0.0)
    def _():
        _route_pass(*refs, index_order=True)


def _route(q, keys, tb):
    t = q.shape[0]
    sel1 = np.zeros((_CAND_N, PEER_TOPK), np.float32)
    sel2 = np.zeros((_CAND_N, PEER_TOPK), np.float32)
    pad = np.full((_CAND_N, 1), NEG_INF, np.float32)
    for row, (a, b) in enumerate(_CAND_PAIRS):
        sel1[row, a] = 1.0
        sel2[row, b] = 1.0
        pad[row, 0] = 0.0
    consts = (jnp.asarray(sel1, BF16), jnp.asarray(sel2, BF16), jnp.asarray(sel1.T, BF16), jnp.asarray(pad))
    row_shp = jax.ShapeDtypeStruct((PEER_HEADS, PEER_N_KEYS, t), F32)
    key_shp = jax.ShapeDtypeStruct((PEER_HEADS, PEER_N_KEYS, t), ROUTE_DTYPE)
    ospec = pl.BlockSpec((PEER_HEADS, PEER_N_KEYS, tb), lambda i: (0, 0, i))
    return pl.pallas_call(
        _route_kernel,
        out_shape=(row_shp, row_shp, key_shp, key_shp),
        grid=(t // tb,),
        in_specs=[pl.BlockSpec((tb, q.shape[1]), lambda i: (i, 0)),
                  pl.BlockSpec(keys.shape, lambda i: (0, 0, 0))]
                 + [pl.BlockSpec(c.shape, lambda i: (0, 0)) for c in consts],
        out_specs=(ospec, ospec, ospec, ospec),
        scratch_shapes=[pltpu.VMEM((2, PEER_TOPK, tb), F32)],
        compiler_params=_params("arbitrary"),
        name="route",
    )(q, keys, *consts)


EXPERT_ROWS = 16
EXPERT_TB = 512
BF16_ROWS = 16


def _experts_kernel(h2_ref, u_ref, vt_ref, a_ref, n1_ref, b_ref, r2_ref, x1_ref, mod_ref, o_ref, acc, hid, act):
    e = pl.program_id(1)

    @pl.when(e == 0)
    def _():
        acc[...] = jnp.zeros_like(acc)

    nk = PEER_N_KEYS
    tb = hid.shape[1]
    hid[...] = _mm_nt(u_ref[...], h2_ref[...])
    for ii in range(EXPERT_ROWS):
        gate = None
        for h in range(PEER_HEADS):
            count = jnp.broadcast_to(n1_ref[h, ii:ii + 1, :], (BF16_ROWS, tb)).astype(BF16)[None]
            scale = jnp.broadcast_to(a_ref[h, ii:ii + 1, :], (BF16_ROWS, tb)).astype(BF16)[None]
            term = jnp.where(r2_ref[h] < count, b_ref[h], jnp.zeros((), BF16)) * scale
            gate = term if gate is None else gate + term
        rows = slice(ii * nk, (ii + 1) * nk)
        act[rows, :] = _gelu(hid[rows, :]).astype(BF16) * gate.reshape(nk, tb)
    acc[...] += _mm(vt_ref[...], act[...])

    @pl.when(e == pl.num_programs(1) - 1)
    def _():
        o_ref[...] = x1_ref[...] + mod_ref[5:6, :] * acc[...].T


def _experts(h2, u_bf, vt_bf, a, n1, b, r2, x1, mod3, seq, tb):
    t, d = h2.shape
    per_seq = seq // tb
    ne = u_bf.shape[0]
    eb = EXPERT_ROWS * PEER_N_KEYS
    tiles = PEER_N_KEYS // BF16_ROWS
    b = b.reshape(PEER_HEADS, tiles, BF16_ROWS, t)
    r2 = r2.reshape(PEER_HEADS, tiles, BF16_ROWS, t)
    rows = pl.BlockSpec((PEER_HEADS, EXPERT_ROWS, tb), lambda i, e: (0, e, i))
    allk = pl.BlockSpec((PEER_HEADS, tiles, BF16_ROWS, tb), lambda i, e: (0, 0, 0, i))
    return pl.pallas_call(
        _experts_kernel,
        out_shape=jax.ShapeDtypeStruct((t, d), F32),
        grid=(t // tb, ne // eb),
        in_specs=[pl.BlockSpec((tb, d), lambda i, e: (i, 0)),
                  pl.BlockSpec((eb, d), lambda i, e: (e, 0)),
                  pl.BlockSpec((d, eb), lambda i, e: (0, e)),
                  rows, rows, allk, allk,
                  pl.BlockSpec((tb, d), lambda i, e: (i, 0)),
                  pl.BlockSpec((None, 6, d), lambda i, e: (i // per_seq, 0, 0))],
        out_specs=pl.BlockSpec((tb, d), lambda i, e: (i, 0)),
        scratch_shapes=[pltpu.VMEM((d, tb), F32), pltpu.VMEM((eb, tb), F32), pltpu.VMEM((eb, tb), BF16)],
        compiler_params=_params("arbitrary", "arbitrary"),
        name="experts",
    )(h2, u_bf, vt_bf, a, n1, b, r2, x1, mod3)


def _regroup_w_in(w):
    d = D_MODEL
    o = 0
    parts = {}
    for name, width in (("aq", ATTN_Q_WIDTH), ("ak", ATTN_KV_WIDTH), ("av", ATTN_KV_WIDTH), ("dq", DN_WIDTH),
                        ("dk", DN_WIDTH), ("dv", DN_WIDTH), ("dz", DN_WIDTH), ("ba", 2 * DN_HEADS),
                        ("ga", d), ("gb", d)):
        parts[name] = w[:, o:o + width]
        o += width
    big = jnp.concatenate([parts[n] for n in ("aq", "dq", "dk", "dv", "dz", "ga", "gb", "ak", "av")], axis=1)
    return big.astype(BF16), parts["ba"]


def _layer(x2, c, seq, w_ada, b_ada, norm1_gain, w_in, attn_q_norm, attn_k_norm, attn_sinks, dn_conv_w, dn_a_log,
           dn_dt_bias, dn_out_norm, w_attn_branch, w_dn_branch, w_out, norm2_gain, peer_w_q, peer_sub_keys,
           peer_u, peer_v):
    d = D_MODEL
    row = lambda a: a.reshape(1, -1)
    tb = min(512, seq)

    mod3 = _adaln(c, w_ada, b_ada).reshape(c.shape[0], 6, d)
    w_big, w_ba = _regroup_w_in(w_in)
    proj, ba, bat = _proj(x2, mod3, row(norm1_gain), w_big, w_ba, seq, tb)
    attn = _attention(proj, row(attn_q_norm), row(attn_k_norm), row(attn_sinks), seq)
    od = _deltanet(proj, ba, bat, dn_conv_w, dn_a_log, dn_dt_bias, dn_out_norm, seq)
    x1, h2, q = _mix(attn, od, proj, x2, mod3, w_attn_branch.astype(BF16), w_dn_branch.astype(BF16),
                     w_out.astype(BF16), row(norm2_gain), peer_w_q.astype(BF16), seq, tb)
    keys = peer_sub_keys.reshape(2 * PEER_HEADS, PEER_N_KEYS, PEER_HALF)
    a, n1, b, r2 = _route(q, keys, min(ROUTE_TB, seq))
    return _experts(h2, peer_u.astype(BF16), peer_v.T.astype(BF16), a, n1, b, r2, x1, mod3, seq, min(EXPERT_TB, seq))


def kernel(x, c, w_ada, b_ada, norm1_gain, w_in, attn_q_norm, attn_k_norm, attn_sinks, dn_conv_w, dn_a_log, dn_dt_bias, dn_out_norm, w_attn_branch, w_dn_branch, w_out, norm2_gain, peer_w_q, peer_sub_keys, peer_u, peer_v):
    b, s, d = x.shape
    x2 = x.reshape(b * s, d)
    for l in range(w_ada.shape[0]):
        x2 = _layer(x2, c, s, w_ada[l], b_ada[l], norm1_gain[l], w_in[l], attn_q_norm[l], attn_k_norm[l],
                    attn_sinks[l], dn_conv_w[l], dn_a_log[l], dn_dt_bias[l], dn_out_norm[l], w_attn_branch[l],
                    w_dn_branch[l], w_out[l], norm2_gain[l], peer_w_q[l], peer_sub_keys[l], peer_u[l], peer_v[l])
    return x2.reshape(b, s, d)
```

```python
import functools

import jax
import jax.numpy as jnp
import numpy as np
from jax import lax
from jax.experimental import pallas as pl
from jax.experimental.pallas import tpu as pltpu

F32 = jnp.float32
BF16 = jnp.bfloat16

D_MODEL = 1024
ATTN_HEADS = 16
ATTN_KV_HEADS = 4
ATTN_HEAD_DIM = 64
ATTN_GROUP = ATTN_HEADS // ATTN_KV_HEADS
ATTN_Q_WIDTH = ATTN_HEADS * ATTN_HEAD_DIM
ATTN_KV_WIDTH = ATTN_KV_HEADS * ATTN_HEAD_DIM
WINDOW = 128
DN_HEADS = 8
DN_DIM = 128
DN_WIDTH = DN_HEADS * DN_DIM
DN_CONV = 4
DN_CHUNK = 64
PEER_HEADS = 8
PEER_N_KEYS = 128
PEER_HALF = 128
PEER_TOPK = 16
EPS = 1e-6
NEG = -1e30
NEG_INF = float("-inf")

VMEM_LIMIT = 48 * 1024 * 1024


def _params(*sem):
    return pltpu.CompilerParams(dimension_semantics=sem, vmem_limit_bytes=VMEM_LIMIT)


def _mm(a, b):
    return jnp.dot(a, b, preferred_element_type=F32)


def _mm_nt(a, b):
    return lax.dot_general(a, b, (((1,), (1,)), ((), ())), preferred_element_type=F32)


def _bmm(a, b):
    return lax.dot_general(a, b, (((2,), (1,)), ((0,), (0,))), preferred_element_type=F32)


def _bmm_nt(a, b):
    return lax.dot_general(a, b, (((2,), (2,)), ((0,), (0,))), preferred_element_type=F32)


def _split(a):
    hi = a.astype(BF16)
    lo = (a - hi.astype(F32)).astype(BF16)
    return hi, lo


def _three_pass(mm, a, b):
    ah, al = _split(a)
    bh, bl = _split(b)
    return mm(ah, bh) + (mm(al, bh) + mm(ah, bl))


def _silu(v):
    return v * jax.nn.sigmoid(v)


def _adaln_kernel(c_ref, w_ref, b_ref, o_ref):
    o_ref[...] = _three_pass(_mm, _silu(c_ref[...]), w_ref[...]) + b_ref[...]


def _adaln(c, w_ada, b_ada):
    b, d = c.shape
    n = w_ada.shape[1] // d
    return pl.pallas_call(
        _adaln_kernel,
        out_shape=jax.ShapeDtypeStruct((b, n * d), F32),
        grid=(n,),
        in_specs=[pl.BlockSpec((b, d), lambda j: (0, 0)),
                  pl.BlockSpec((d, d), lambda j: (0, j)),
                  pl.BlockSpec((1, d), lambda j: (0, j))],
        out_specs=pl.BlockSpec((b, d), lambda j: (0, j)),
        compiler_params=_params("arbitrary"),
        name="adaln",
    )(c, w_ada, b_ada.reshape(1, -1))


PROJ_COLS = 1536
PROJ_TB = 1024
MIX_TB = 512
COL_AQ, COL_DQ, COL_DK, COL_DV, COL_DZ, COL_GA, COL_GB = range(7)
COL_AK = 7 * D_MODEL // ATTN_KV_WIDTH
COL_AV = COL_AK + 1
PROJ_WIDTH = 7 * D_MODEL + 2 * ATTN_KV_WIDTH


def _proj_kernel(x_ref, mod_ref, g_ref, w_ref, wba_ref, o_ref, ba_ref, bat_ref, h_scr):
    @pl.when(pl.program_id(1) == 0)
    def _():
        x = x_ref[...]
        y = x * lax.rsqrt(jnp.mean(x * x, axis=-1, keepdims=True) + EPS) * g_ref[...]
        mod = mod_ref[...]
        h = y * (1.0 + mod[1:2]) + mod[0:1]
        h_scr[...] = h.astype(BF16)
        ba = _three_pass(_mm, h, wba_ref[...])
        ba_ref[...] = ba
        n = ba.shape[1]
        eye = jnp.where(lax.broadcasted_iota(jnp.int32, (n, n), 0) == lax.broadcasted_iota(jnp.int32, (n, n), 1),
                        1.0, 0.0).astype(BF16)
        p0 = ba.astype(BF16)
        r1 = ba - p0.astype(F32)
        p1 = r1.astype(BF16)
        p2 = (r1 - p1.astype(F32)).astype(BF16)
        bat_ref[...] = _mm_nt(eye, p0) + (_mm_nt(eye, p1) + _mm_nt(eye, p2))

    o_ref[...] = _mm(h_scr[...], w_ref[...]).astype(BF16)


def _proj(x2, mod3, gain, w_big, w_ba, seq, tb):
    t, d = x2.shape
    per_seq = seq // tb
    nba = w_ba.shape[1]
    return pl.pallas_call(
        _proj_kernel,
        out_shape=(jax.ShapeDtypeStruct((t, PROJ_WIDTH), BF16),
                   jax.ShapeDtypeStruct((t, nba), F32),
                   jax.ShapeDtypeStruct((nba, t), F32)),
        grid=(t // tb, PROJ_WIDTH // PROJ_COLS),
        in_specs=[pl.BlockSpec((tb, d), lambda i, j: (i, 0)),
                  pl.BlockSpec((None, 6, d), lambda i, j: (i // per_seq, 0, 0)),
                  pl.BlockSpec((1, d), lambda i, j: (0, 0)),
                  pl.BlockSpec((d, PROJ_COLS), lambda i, j: (0, j)),
                  pl.BlockSpec((d, nba), lambda i, j: (0, 0))],
        out_specs=(pl.BlockSpec((tb, PROJ_COLS), lambda i, j: (i, j)),
                   pl.BlockSpec((tb, nba), lambda i, j: (i, 0)),
                   pl.BlockSpec((nba, tb), lambda i, j: (0, i))),
        scratch_shapes=[pltpu.VMEM((tb, d), BF16)],
        compiler_params=_params("arbitrary", "arbitrary"),
        name="proj",
    )(x2, mod3, gain, w_big, w_ba)


ATTN_TQ = 256
LANES = 128


def _head_scale(x, seg, segt):
    hi, lo = _split(x * x)
    ss = _mm(hi, seg) + _mm(lo, seg)
    rh, rl = _split(lax.rsqrt(ss * (1.0 / ATTN_HEAD_DIM) + EPS))
    return _mm(rh, segt) + _mm(rl, segt)


def _attn_kernel(q_ref, kc_ref, vc_ref, kp_ref, vp_ref, qg_ref, kg_ref, sink_ref, seg_ref, segt_ref, o_ref,
                 *, steps_per_seq):
    blk = WINDOW
    tq = q_ref.shape[0]
    first = (pl.program_id(0) % steps_per_seq) == 0
    seg, segt = seg_ref[...], segt_ref[...]
    kvw = ATTN_KV_WIDTH

    q = q_ref[...].astype(F32)
    qn = (q * _head_scale(q, seg, segt) * (qg_ref[...] * (ATTN_HEAD_DIM ** -0.5))).astype(BF16)
    k = jnp.concatenate([kp_ref[...], kc_ref[...]], axis=0).astype(F32)
    kn = k * _head_scale(k, seg[:kvw], segt[:, :kvw]) * kg_ref[...]
    v = jnp.concatenate([vp_ref[...], vc_ref[...]], axis=0).astype(F32)

    left = lax.broadcasted_iota(jnp.int32, (blk + tq, LANES), 1) < ATTN_HEAD_DIM

    def halves(x):
        out = []
        for tile in range(kvw // LANES):
            xt = x[:, tile * LANES:(tile + 1) * LANES]
            xr = pltpu.roll(xt, ATTN_HEAD_DIM, 1)
            out.append((jnp.where(left, xt, 0.0).astype(BF16), jnp.where(left, 0.0, xr).astype(BF16)))
            out.append((jnp.where(left, xr, 0.0).astype(BF16), jnp.where(left, 0.0, xt).astype(BF16)))
        return out

    k_parts, v_parts = halves(kn), halves(v)

    keys = 2 * blk
    qi = lax.broadcasted_iota(jnp.int32, (blk, 2 * keys), 0)
    kj = lax.broadcasted_iota(jnp.int32, (blk, 2 * keys), 1) & (keys - 1)
    rel = qi + blk - kj
    band = (rel >= 0) & (rel < WINDOW)
    col = lax.broadcasted_iota(jnp.int32, (1, 2 * keys), 1)
    sinks = sink_ref[...]
    row = lax.broadcasted_iota(jnp.int32, (2 * keys, LANES), 0)
    lane = lax.broadcasted_iota(jnp.int32, (2 * keys, LANES), 1)
    sink_row = (row & (keys - 1)) == 0
    ones_blk = jnp.where((row < keys) == (lane < ATTN_HEAD_DIM), 1.0, 0.0).astype(BF16)
    pairs = ATTN_HEADS // 2
    per_kv = ATTN_GROUP // 2

    for sb in range(tq // blk):
        krows = slice(sb * blk, sb * blk + keys)
        qrows = slice(sb * blk, (sb + 1) * blk)
        mask = band & (jnp.logical_not(first) | (kj >= blk)) if sb == 0 else band
        kblk = [jnp.concatenate([a[krows], b[krows]], axis=0) for a, b in k_parts]
        vblk = [jnp.where(sink_row, jnp.zeros((), BF16), jnp.concatenate([a[krows], b[krows]], axis=0))
                for a, b in v_parts]
        scores = [_mm_nt(qn[qrows, j * LANES:(j + 1) * LANES], kblk[j // per_kv]) for j in range(pairs)]
        fills = [jnp.where(col == 0, sinks[0:1, 2 * j:2 * j + 1],
                           jnp.where(col == keys, sinks[0:1, 2 * j + 1:2 * j + 2], NEG)) for j in range(pairs)]
        logits = [jnp.where(mask, scores[j], fills[j]) for j in range(pairs)]
        tops = [[jnp.max(s[:, u * keys:(u + 1) * keys], axis=-1, keepdims=True) for u in range(2)] for s in logits]
        probs = [jnp.exp(s - jnp.concatenate([jnp.broadcast_to(m, (blk, keys)) for m in ms], axis=1)).astype(BF16)
                 for s, ms in zip(logits, tops)]
        outs = [_mm(p, vblk[j // per_kv]) for j, p in enumerate(probs)]
        dens = [_mm(p, ones_blk) for p in probs]
        for j in range(pairs):
            o_ref[qrows, j * LANES:(j + 1) * LANES] = (outs[j] / dens[j]).astype(BF16)


def _attention(proj, qn, kn, sinks, seq):
    t = proj.shape[0]
    tq = min(ATTN_TQ, seq)
    per_seq = seq // tq
    halo = tq // WINDOW
    prev = lambda i: jnp.maximum(i * halo - 1, 0)
    head_of_lane = np.arange(ATTN_Q_WIDTH) // ATTN_HEAD_DIM
    seg = (head_of_lane[:, None] == np.arange(ATTN_HEADS)[None, :]).astype(np.float32)
    consts = (jnp.tile(qn, (1, ATTN_HEADS)), jnp.tile(kn, (1, ATTN_KV_HEADS)), sinks,
              jnp.asarray(seg, BF16), jnp.asarray(seg.T, BF16))
    return pl.pallas_call(
        functools.partial(_attn_kernel, steps_per_seq=per_seq),
        out_shape=jax.ShapeDtypeStruct((t, ATTN_Q_WIDTH), BF16),
        grid=(t // tq,),
        in_specs=[pl.BlockSpec((tq, ATTN_Q_WIDTH), lambda i: (i, COL_AQ)),
                  pl.BlockSpec((tq, ATTN_KV_WIDTH), lambda i: (i, COL_AK)),
                  pl.BlockSpec((tq, ATTN_KV_WIDTH), lambda i: (i, COL_AV)),
                  pl.BlockSpec((WINDOW, ATTN_KV_WIDTH), lambda i: (prev(i), COL_AK)),
                  pl.BlockSpec((WINDOW, ATTN_KV_WIDTH), lambda i: (prev(i), COL_AV))]
                 + [pl.BlockSpec(c.shape, lambda i: (0, 0)) for c in consts],
        out_specs=pl.BlockSpec((tq, ATTN_Q_WIDTH), lambda i: (i, 0)),
        compiler_params=_params("arbitrary"),
        name="attn",
    )(proj, proj, proj, proj, proj, *consts)


DN_BLOCK = 128
CONV_HALO = 8


def _dn_kernel(q_ref, k_ref, v_ref, pq_ref, pk_ref, pv_ref, z_ref, ba_ref, bat_ref, cw_ref,
               alog_r, dt_r, alog_c, dt_c, on_ref, o_ref, state, *, blocks_per_seq):
    first = (pl.program_id(0) % blocks_per_seq) == 0
    cb, cl, nh = DN_BLOCK, DN_CHUNK, DN_HEADS

    @pl.when(first)
    def _():
        state[...] = jnp.zeros_like(state)

    def conv(cur_ref, prev_ref, part):
        cur = cur_ref[...].astype(F32)
        prev = jnp.where(first, 0.0, prev_ref[...].astype(F32))
        xc = jnp.concatenate([prev, cur], axis=0)
        acc = None
        for j in range(DN_CONV):
            w = cw_ref[j:j + 1, part * DN_WIDTH:(part + 1) * DN_WIDTH]
            shift = DN_CONV - 1 - j
            xs = pltpu.roll(xc, shift, 0) if shift else xc
            term = w * xs[CONV_HALO:CONV_HALO + cb]
            acc = term if acc is None else acc + term
        return _silu(acc)

    qc, kc, vc = conv(q_ref, pq_ref, 0), conv(k_ref, pk_ref, 1), conv(v_ref, pv_ref, 2)

    ba = ba_ref[...]
    bat = bat_ref[...]
    beta = jax.nn.sigmoid(ba[:, :nh])
    g_col = -jnp.exp(alog_r[...]) * jax.nn.softplus(ba[:, nh:] + dt_r[...])
    g_row = -jnp.exp(alog_c[...]) * jax.nn.softplus(bat[nh:] + dt_c[...])

    r = lax.broadcasted_iota(jnp.int32, (cb, cb), 0)
    c = lax.broadcasted_iota(jnp.int32, (cb, cb), 1)
    same = (r // cl) == (c // cl)
    lower = jnp.where(same & (c <= r), 1.0, 0.0).astype(BF16)
    upper = jnp.where(same & (r <= c), 1.0, 0.0).astype(BF16)

    def exact3(mm, a, b, split_left):
        v0 = a if split_left else b
        p0 = v0.astype(BF16)
        r1 = v0 - p0.astype(F32)
        p1 = r1.astype(BF16)
        p2 = (r1 - p1.astype(F32)).astype(BF16)
        if split_left:
            return mm(p0, b) + (mm(p1, b) + mm(p2, b))
        return mm(a, p0) + (mm(a, p1) + mm(a, p2))

    gc_col = exact3(_mm, lower, g_col, False)
    gc_row = exact3(_mm, g_row, upper, True)

    ri = lax.broadcasted_iota(jnp.int32, (cl, cl), 0)
    ci = lax.broadcasted_iota(jnp.int32, (cl, cl), 1)
    tril = ci <= ri
    strict = ci < ri
    eye = jnp.where(ci == ri, 1.0, 0.0)
    eye_k = jnp.where(lax.broadcasted_iota(jnp.int32, (DN_DIM, DN_DIM), 0)
                      == lax.broadcasted_iota(jnp.int32, (DN_DIM, DN_DIM), 1), 1.0, 0.0).astype(BF16)

    z = z_ref[...].astype(F32)
    for ch in range(cb // cl):
        t0 = ch * cl
        qs, ks, kbs, kgs, vbs, decs, qgs, kds, gls = [], [], [], [], [], [], [], [], []
        for h in range(nh):
            sl = slice(h * DN_DIM, (h + 1) * DN_DIM)
            qh = qc[t0:t0 + cl, sl]
            kh = kc[t0:t0 + cl, sl]
            vh = vc[t0:t0 + cl, sl]
            qh = qh * lax.rsqrt(jnp.sum(qh * qh, axis=-1, keepdims=True) + EPS) * (DN_DIM ** -0.5)
            kh = kh * lax.rsqrt(jnp.sum(kh * kh, axis=-1, keepdims=True) + EPS)
            bh = beta[t0:t0 + cl, h:h + 1]
            gcol = gc_col[t0:t0 + cl, h:h + 1]
            grow = gc_row[h:h + 1, t0:t0 + cl]
            glast = gc_col[t0 + cl - 1:t0 + cl, h:h + 1]
            dec = jnp.where(tril, jnp.exp(jnp.where(tril, gcol - grow, 0.0)), 0.0)
            eg = jnp.exp(gcol)
            kb = kh * bh
            qs.append(qh); ks.append(kh); kbs.append(kb); kgs.append(kb * eg)
            vbs.append(vh * bh); decs.append(dec); qgs.append(qh * eg)
            kds.append(kh * jnp.exp(glast - gcol)); gls.append(jnp.exp(glast))
        q3, k3 = jnp.stack(qs).astype(BF16), jnp.stack(ks).astype(BF16)
        kb3, kg3, vb3 = jnp.stack(kbs).astype(BF16), jnp.stack(kgs).astype(BF16), jnp.stack(vbs).astype(BF16)
        dec3, qg3, kd3 = jnp.stack(decs), jnp.stack(qgs).astype(BF16), jnp.stack(kds).astype(BF16)

        kq = _bmm_nt(jnp.concatenate([kb3, q3], axis=1), k3)
        a_low = jnp.where(strict[None], kq[:, :cl] * dec3, 0.0)
        qk3 = kq[:, cl:] * dec3
        pw = -a_low
        t_mat = eye[None] + pw
        pw = _three_pass(_bmm, pw, pw)
        for m in range(1, 6):
            both = jnp.concatenate([pw, t_mat], axis=1)
            prod = _three_pass(_bmm, both, pw)
            pw = prod[:, :cl]
            t_mat = t_mat + prod[:, cl:]
        uw = _bmm(t_mat.astype(BF16), jnp.concatenate([vb3, kg3], axis=2))
        u3, w3 = uw[:, :, :DN_DIM], uw[:, :, DN_DIM:]

        s_old = state[...]
        s_bf = s_old.astype(BF16)
        ws_qs = _bmm(jnp.concatenate([w3.astype(BF16), qg3], axis=1), s_bf)
        v_new = u3 - ws_qs[:, :cl]
        o3 = ws_qs[:, cl:] + _bmm(qk3.astype(BF16), v_new.astype(BF16))
        kdt = _bmm_nt(jnp.broadcast_to(eye_k[None], (nh, DN_DIM, DN_DIM)), kd3)
        upd = _bmm(kdt.astype(BF16), v_new.astype(BF16))
        state[...] = jnp.stack([s_old[h] * gls[h] for h in range(nh)]) + upd

        for h in range(nh):
            sl = slice(h * DN_DIM, (h + 1) * DN_DIM)
            oh = o3[h]
            oh = oh * lax.rsqrt(jnp.mean(oh * oh, axis=-1, keepdims=True) + EPS) * on_ref[...]
            o_ref[t0:t0 + cl, sl] = (oh * _silu(z[t0:t0 + cl, sl])).astype(BF16)


def _deltanet(proj, ba, bat, conv_w, a_log, dt_bias, out_norm, seq):
    t = proj.shape[0]
    cb = DN_BLOCK
    per_seq = seq // cb
    halo = cb // CONV_HALO
    prev = lambda i: jnp.maximum(i * halo - 1, 0)
    nba = ba.shape[1]
    row = lambda a: a.reshape(1, -1)
    col = lambda a: a.reshape(-1, 1)
    small = lambda shp: pl.BlockSpec(shp, lambda i: (0, 0))
    return pl.pallas_call(
        functools.partial(_dn_kernel, blocks_per_seq=per_seq),
        out_shape=jax.ShapeDtypeStruct((t, DN_WIDTH), BF16),
        grid=(t // cb,),
        in_specs=[pl.BlockSpec((cb, DN_WIDTH), lambda i: (i, COL_DQ)),
                  pl.BlockSpec((cb, DN_WIDTH), lambda i: (i, COL_DK)),
                  pl.BlockSpec((cb, DN_WIDTH), lambda i: (i, COL_DV)),
                  pl.BlockSpec((CONV_HALO, DN_WIDTH), lambda i: (prev(i), COL_DQ)),
                  pl.BlockSpec((CONV_HALO, DN_WIDTH), lambda i: (prev(i), COL_DK)),
                  pl.BlockSpec((CONV_HALO, DN_WIDTH), lambda i: (prev(i), COL_DV)),
                  pl.BlockSpec((cb, DN_WIDTH), lambda i: (i, COL_DZ)),
                  pl.BlockSpec((cb, nba), lambda i: (i, 0)),
                  pl.BlockSpec((nba, cb), lambda i: (0, i)),
                  small((DN_CONV, 3 * DN_WIDTH)),
                  small((1, DN_HEADS)), small((1, DN_HEADS)),
                  small((DN_HEADS, 1)), small((DN_HEADS, 1)),
                  small((1, DN_DIM))],
        out_specs=pl.BlockSpec((cb, DN_WIDTH), lambda i: (i, 0)),
        scratch_shapes=[pltpu.VMEM((DN_HEADS, DN_DIM, DN_DIM), F32)],
        compiler_params=_params("arbitrary"),
        name="deltanet",
    )(proj, proj, proj, proj, proj, proj, proj, ba, bat, conv_w,
      row(a_log), row(dt_bias), col(a_log), col(dt_bias), row(out_norm))


def _mix_kernel(att_ref, od_ref, ga_ref, gb_ref, x_ref, mod_ref, wa_ref, wb_ref, wo_ref, g2_ref,
                wq_ref, x1_ref, h2_ref, q_ref):
    ya = _mm(att_ref[...], wa_ref[...])
    yb = _mm(od_ref[...], wb_ref[...])
    mixed = jax.nn.sigmoid(ga_ref[...].astype(F32)) * ya + jax.nn.sigmoid(gb_ref[...].astype(F32)) * yb
    mod = mod_ref[...]
    x1 = x_ref[...] + mod[2:3] * _mm(mixed.astype(BF16), wo_ref[...])
    x1_ref[...] = x1
    y = x1 * lax.rsqrt(jnp.mean(x1 * x1, axis=-1, keepdims=True) + EPS) * g2_ref[...]
    h2 = y * (1.0 + mod[4:5]) + mod[3:4]
    h2_bf = h2.astype(BF16)
    h2_ref[...] = h2_bf
    q_ref[...] = _mm(h2_bf, wq_ref[...])


def _mix(attn, od, proj, x2, mod3, wa, wb, wo, gain2, wq, seq, tb):
    t, d = x2.shape
    per_seq = seq // tb
    nq = wq.shape[1]
    full = lambda shp: pl.BlockSpec(shp, lambda i: (0, 0))
    return pl.pallas_call(
        _mix_kernel,
        out_shape=(jax.ShapeDtypeStruct((t, d), F32),
                   jax.ShapeDtypeStruct((t, d), BF16),
                   jax.ShapeDtypeStruct((t, nq), F32)),
        grid=(t // tb,),
        in_specs=[pl.BlockSpec((tb, d), lambda i: (i, 0)),
                  pl.BlockSpec((tb, d), lambda i: (i, 0)),
                  pl.BlockSpec((tb, d), lambda i: (i, COL_GA)),
                  pl.BlockSpec((tb, d), lambda i: (i, COL_GB)),
                  pl.BlockSpec((tb, d), lambda i: (i, 0)),
                  pl.BlockSpec((None, 6, d), lambda i: (i // per_seq, 0, 0)),
                  full((d, d)), full((d, d)), full((d, d)), full((1, d)),
                  full((d, nq))],
        out_specs=(pl.BlockSpec((tb, d), lambda i: (i, 0)),
                   pl.BlockSpec((tb, d), lambda i: (i, 0)),
                   pl.BlockSpec((tb, nq), lambda i: (i, 0))),
        compiler_params=_params("arbitrary"),
        name="mix",
    )(attn, od, proj, proj, x2, mod3, wa, wb, wo, gain2, wq)


ROUTE_DTYPE = BF16
ROUTE_TB = 256

_CAND_ROWS = tuple(PEER_TOPK // (a + 1) for a in range(PEER_TOPK))


def _scores(q_ref, keys_ref, hp):
    return _three_pass(_mm_nt, keys_ref[hp], q_ref[:, hp * PEER_HALF:(hp + 1) * PEER_HALF])


_CAND_PAIRS = tuple((a, b) for a in range(PEER_TOPK) for b in range(_CAND_ROWS[a]))
_CAND_PAD = -len(_CAND_PAIRS) % 8
_CAND_N = len(_CAND_PAIRS) + _CAND_PAD


def _select_rows(onehot, vals):
    p0 = vals.astype(BF16)
    r1 = vals - p0.astype(F32)
    p1 = r1.astype(BF16)
    p2 = (r1 - p1.astype(F32)).astype(BF16)
    return _mm(onehot, p0) + (_mm(onehot, p1) + _mm(onehot, p2))


def _merge_network(lo, hi, r):
    step = r * 2
    if step < hi - lo:
        yield from _merge_network(lo, hi, step)
        yield from _merge_network(lo + r, hi, step)
        yield from ((i, i + r) for i in range(lo + r, hi - r, step))
    else:
        yield (lo, lo + r)


def _sort_network(lo, hi):
    if hi > lo:
        mid = lo + (hi - lo) // 2
        yield from _sort_network(lo, mid)
        yield from _sort_network(mid + 1, hi)
        yield from _merge_network(lo, hi, 1)


SUBLANES = 8
_KEY_GROUPS = PEER_N_KEYS // SUBLANES
_SORT_KEY_GROUPS = tuple(_sort_network(0, _KEY_GROUPS - 1))


def _top_values_distinct(scores, ts_scr):
    tb = scores[0].shape[1]
    bad = jnp.zeros((1, tb), F32)
    lists = []
    for s in scores:
        ls = [s[g * SUBLANES:(g + 1) * SUBLANES, :] for g in range(_KEY_GROUPS)]
        for a, b in _SORT_KEY_GROUPS:
            ls[a], ls[b] = jnp.maximum(ls[a], ls[b]), jnp.minimum(ls[a], ls[b])
        lists.append(ls)
    taken = [jnp.zeros((SUBLANES, tb), F32) for _ in scores]
    prev = [None for _ in scores]
    for r in range(PEER_TOPK + 1):
        for p, ls in enumerate(lists):
            m = jnp.max(ls[0], axis=0, keepdims=True)
            win = ls[0] == m
            taken[p] = taken[p] + jnp.where(win, 1.0, 0.0)
            for g in range(PEER_TOPK - r):
                below = ls[g + 1] if g + 1 < _KEY_GROUPS else NEG_INF
                ls[g] = jnp.where(win, below, ls[g])
            if r < PEER_TOPK:
                ts_scr[p, r:r + 1, :] = m
            if prev[p] is not None:
                bad = jnp.maximum(bad, jnp.where(m >= prev[p], 1.0, 0.0))
            prev[p] = m
    for t in taken:
        bad = jnp.maximum(bad, jnp.where(jnp.sum(t, axis=0, keepdims=True) != PEER_TOPK + 1.0, 1.0, 0.0))
    return bad


def _route_pass(q_ref, keys_ref, sel1_ref, sel2_ref, cnt_ref, pad_ref, a_ref, n1_ref, b_ref, r2_ref, ts_scr,
                *, index_order):
    tb = q_ref.shape[0]
    topk = float(PEER_TOPK)
    bad = jnp.zeros((1, tb), F32)
    cand_row = lax.broadcasted_iota(jnp.int32, (_CAND_N, tb), 0)
    key_row = lax.broadcasted_iota(jnp.int32, (PEER_N_KEYS, tb), 0)
    for h in range(PEER_HEADS):
        sc = [_scores(q_ref, keys_ref, 2 * h + p) for p in range(2)]
        rank = [jnp.full(s.shape, topk, F32) for s in sc]
        if index_order:
            work = list(sc)
            for r in range(PEER_TOPK):
                for p in range(2):
                    m = jnp.max(work[p], axis=0, keepdims=True)
                    hit = key_row == jnp.min(jnp.where(work[p] == m, key_row, PEER_N_KEYS), axis=0, keepdims=True)
                    rank[p] = jnp.where(hit, float(r), rank[p])
                    work[p] = jnp.where(hit, NEG_INF, work[p])
                    ts_scr[p, r:r + 1, :] = m
        else:
            bad = jnp.maximum(bad, _top_values_distinct(sc, ts_scr))
        ts1, ts2 = ts_scr[0], ts_scr[1]
        if not index_order:
            rank[1] = functools.reduce(lambda acc, r: acc + jnp.where(ts2[r:r + 1] > sc[1], 1.0, 0.0),
                                       range(PEER_TOPK), jnp.zeros(sc[1].shape, F32))
        cand = (_select_rows(sel1_ref[...], ts1) + _select_rows(sel2_ref[...], ts2)) + pad_ref[...]
        m0 = ts1[0:1] + ts2[0:1]
        zsum = jnp.zeros_like(m0)
        for r in range(PEER_TOPK):
            m = jnp.max(cand, axis=0, keepdims=True)
            first = jnp.min(jnp.where(cand == m, cand_row, _CAND_N), axis=0, keepdims=True)
            cand = jnp.where(cand_row == first, NEG_INF, cand)
            zsum = zsum + jnp.exp(m - m0)
        gone = jnp.where(cand == NEG_INF, 1.0, 0.0)
        per_rank = _mm(cnt_ref[...], gone.astype(BF16))
        n1 = jnp.zeros(sc[0].shape, F32)
        for a in range(PEER_TOPK):
            is_rank_a = rank[0] == float(a) if index_order else sc[0] == ts1[a:a + 1]
            n1 = jnp.where(is_rank_a, per_rank[a:a + 1], n1)
        in1 = rank[0] < topk if index_order else sc[0] >= ts1[PEER_TOPK - 1:PEER_TOPK]
        in2 = rank[1] < topk
        a_ref[h] = jnp.where(in1, jnp.exp(jnp.where(in1, sc[0] - ts1[0:1], 0.0)), 0.0)
        n1_ref[h] = n1
        b_ref[h] = jnp.where(in2, jnp.exp(jnp.where(in2, sc[1] - ts2[0:1], 0.0)) / zsum, 0.0).astype(ROUTE_DTYPE)
        r2_ref[h] = rank[1].astype(ROUTE_DTYPE)
    return bad


def _route_kernel(*refs):
    bad = _route_pass(*refs, index_order=False)

    @pl.when(jnp.max(bad) > 0.0)
    def _():
        _route_pass(*refs, index_order=True)


def _route(q, keys, tb):
    t = q.shape[0]
    sel1 = np.zeros((_CAND_N, PEER_TOPK), np.float32)
    sel2 = np.zeros((_CAND_N, PEER_TOPK), np.float32)
    pad = np.full((_CAND_N, 1), NEG_INF, np.float32)
    for row, (a, b) in enumerate(_CAND_PAIRS):
        sel1[row, a] = 1.0
        sel2[row, b] = 1.0
        pad[row, 0] = 0.0
    consts = (jnp.asarray(sel1, BF16), jnp.asarray(sel2, BF16), jnp.asarray(sel1.T, BF16), jnp.asarray(pad))
    row_shp = jax.ShapeDtypeStruct((PEER_HEADS, PEER_N_KEYS, t), F32)
    key_shp = jax.ShapeDtypeStruct((PEER_HEADS, PEER_N_KEYS, t), ROUTE_DTYPE)
    ospec = pl.BlockSpec((PEER_HEADS, PEER_N_KEYS, tb), lambda i: (0, 0, i))
    return pl.pallas_call(
        _route_kernel,
        out_shape=(row_shp, row_shp, key_shp, key_shp),
        grid=(t // tb,),
        in_specs=[pl.BlockSpec((tb, q.shape[1]), lambda i: (i, 0)),
                  pl.BlockSpec(keys.shape, lambda i: (0, 0, 0))]
                 + [pl.BlockSpec(c.shape, lambda i: (0, 0)) for c in consts],
        out_specs=(ospec, ospec, ospec, ospec),
        scratch_shapes=[pltpu.VMEM((2, PEER_TOPK, tb), F32)],
        compiler_params=_params("arbitrary"),
        name="route",
    )(q, keys, *consts)


EXPERT_ROWS = 16
EXPERT_TB = 512
BF16_ROWS = 16
EXPERT_SLICES = 8


def _experts_kernel(h2_ref, u_ref, vt_ref, a_ref, n1_ref, b_ref, r2_ref, x1_ref, mod_ref, o_ref, acc, act):
    e = pl.program_id(1)

    @pl.when(e == 0)
    def _():
        acc[...] = jnp.zeros_like(acc)

    nk = PEER_N_KEYS
    tb = acc.shape[1]
    per = EXPERT_ROWS // EXPERT_SLICES
    rows_of = lambda q: slice(q * per * nk, (q + 1) * per * nk)
    hid = [None] * EXPERT_SLICES
    hid[0] = _mm_nt(u_ref[rows_of(0), :], h2_ref[...])
    for q in range(EXPERT_SLICES):
        if q + 1 < EXPERT_SLICES:
            hid[q + 1] = _mm_nt(u_ref[rows_of(q + 1), :], h2_ref[...])
        for k in range(per):
            ii = q * per + k
            gate = None
            for h in range(PEER_HEADS):
                count = jnp.broadcast_to(n1_ref[h, ii:ii + 1, :], (BF16_ROWS, tb)).astype(BF16)[None]
                scale = jnp.broadcast_to(0.5 * a_ref[h, ii:ii + 1, :], (BF16_ROWS, tb)).astype(BF16)[None]
                term = jnp.where(r2_ref[h] < count, b_ref[h], jnp.zeros((), BF16)) * scale
                gate = term if gate is None else gate + term
            v = hid[q][k * nk:(k + 1) * nk, :]
            act[ii * nk:(ii + 1) * nk, :] = ((v * (1.0 + lax.erf(v * (2.0 ** -0.5)))).astype(BF16)
                                             * gate.reshape(nk, tb))
        if q % 2 == 1:
            er = slice((q - 1) * per * nk, (q + 1) * per * nk)
            acc[...] += _mm(vt_ref[:, er], act[er, :])

    @pl.when(e == pl.num_programs(1) - 1)
    def _():
        o_ref[...] = x1_ref[...] + mod_ref[5:6, :] * acc[...].T


def _experts(h2, u_bf, vt_bf, a, n1, b, r2, x1, mod3, seq, tb):
    t, d = h2.shape
    per_seq = seq // tb
    ne = u_bf.shape[0]
    eb = EXPERT_ROWS * PEER_N_KEYS
    tiles = PEER_N_KEYS // BF16_ROWS
    b = b.reshape(PEER_HEADS, tiles, BF16_ROWS, t)
    r2 = r2.reshape(PEER_HEADS, tiles, BF16_ROWS, t)
    rows = pl.BlockSpec((PEER_HEADS, EXPERT_ROWS, tb), lambda i, e: (0, e, i))
    allk = pl.BlockSpec((PEER_HEADS, tiles, BF16_ROWS, tb), lambda i, e: (0, 0, 0, i))
    return pl.pallas_call(
        _experts_kernel,
        out_shape=jax.ShapeDtypeStruct((t, d), F32),
        grid=(t // tb, ne // eb),
        in_specs=[pl.BlockSpec((tb, d), lambda i, e: (i, 0)),
                  pl.BlockSpec((eb, d), lambda i, e: (e, 0)),
                  pl.BlockSpec((d, eb), lambda i, e: (0, e)),
                  rows, rows, allk, allk,
                  pl.BlockSpec((tb, d), lambda i, e: (i, 0)),
                  pl.BlockSpec((None, 6, d), lambda i, e: (i // per_seq, 0, 0))],
        out_specs=pl.BlockSpec((tb, d), lambda i, e: (i, 0)),
        scratch_shapes=[pltpu.VMEM((d, tb), F32), pltpu.VMEM((eb, tb), BF16)],
        compiler_params=_params("arbitrary", "arbitrary"),
        name="experts",
    )(h2, u_bf, vt_bf, a, n1, b, r2, x1, mod3)


def _regroup_w_in(w):
    d = D_MODEL
    o = 0
    parts = {}
    for name, width in (("aq", ATTN_Q_WIDTH), ("ak", ATTN_KV_WIDTH), ("av", ATTN_KV_WIDTH), ("dq", DN_WIDTH),
                        ("dk", DN_WIDTH), ("dv", DN_WIDTH), ("dz", DN_WIDTH), ("ba", 2 * DN_HEADS),
                        ("ga", d), ("gb", d)):
        parts[name] = w[:, o:o + width]
        o += width
    big = jnp.concatenate([parts[n] for n in ("aq", "dq", "dk", "dv", "dz", "ga", "gb", "ak", "av")], axis=1)
    return big.astype(BF16), parts["ba"]


def _layer(x2, c, seq, w_ada, b_ada, norm1_gain, w_in, attn_q_norm, attn_k_norm, attn_sinks, dn_conv_w, dn_a_log,
           dn_dt_bias, dn_out_norm, w_attn_branch, w_dn_branch, w_out, norm2_gain, peer_w_q, peer_sub_keys,
           peer_u, peer_v):
    d = D_MODEL
    row = lambda a: a.reshape(1, -1)
    tb = min(MIX_TB, seq)

    mod3 = _adaln(c, w_ada, b_ada).reshape(c.shape[0], 6, d)
    w_big, w_ba = _regroup_w_in(w_in)
    proj, ba, bat = _proj(x2, mod3, row(norm1_gain), w_big, w_ba, seq, min(PROJ_TB, seq))
    attn = _attention(proj, row(attn_q_norm), row(attn_k_norm), row(attn_sinks), seq)
    od = _deltanet(proj, ba, bat, dn_conv_w, dn_a_log, dn_dt_bias, dn_out_norm, seq)
    x1, h2, q = _mix(attn, od, proj, x2, mod3, w_attn_branch.astype(BF16), w_dn_branch.astype(BF16),
                     w_out.astype(BF16), row(norm2_gain), peer_w_q.astype(BF16), seq, tb)
    keys = peer_sub_keys.reshape(2 * PEER_HEADS, PEER_N_KEYS, PEER_HALF)
    a, n1, b, r2 = _route(q, keys, min(ROUTE_TB, seq))
    return _experts(h2, peer_u.astype(BF16), peer_v.T.astype(BF16), a, n1, b, r2, x1, mod3, seq, min(EXPERT_TB, seq))


def kernel(x, c, w_ada, b_ada, norm1_gain, w_in, attn_q_norm, attn_k_norm, attn_sinks, dn_conv_w, dn_a_log, dn_dt_bias, dn_out_norm, w_attn_branch, w_dn_branch, w_out, norm2_gain, peer_w_q, peer_sub_keys, peer_u, peer_v):
    b, s, d = x.shape
    x2 = x.reshape(b * s, d)
    for l in range(w_ada.shape[0]):
        x2 = _layer(x2, c, s, w_ada[l], b_ada[l], norm1_gain[l], w_in[l], attn_q_norm[l], attn_k_norm[l],
                    attn_sinks[l], dn_conv_w[l], dn_a_log[l], dn_dt_bias[l], dn_out_norm[l], w_attn_branch[l],
                    w_dn_branch[l], w_out[l], norm2_gain[l], peer_w_q[l], peer_sub_keys[l], peer_u[l], peer_v[l])
    return x2.reshape(b, s, d)
```

```python
import functools

import jax
import jax.numpy as jnp
import numpy as np
from jax import lax
from jax.experimental import pallas as pl
from jax.experimental.pallas import tpu as pltpu

F32 = jnp.float32
BF16 = jnp.bfloat16

D_MODEL = 1024
ATTN_HEADS = 16
ATTN_KV_HEADS = 4
ATTN_HEAD_DIM = 64
ATTN_GROUP = ATTN_HEADS // ATTN_KV_HEADS
ATTN_Q_WIDTH = ATTN_HEADS * ATTN_HEAD_DIM
ATTN_KV_WIDTH = ATTN_KV_HEADS * ATTN_HEAD_DIM
WINDOW = 128
DN_HEADS = 8
DN_DIM = 128
DN_WIDTH = DN_HEADS * DN_DIM
DN_CONV = 4
DN_CHUNK = 64
PEER_HEADS = 8
PEER_N_KEYS = 128
PEER_HALF = 128
PEER_TOPK = 16
EPS = 1e-6
NEG = -1e30
NEG_INF = float("-inf")

VMEM_LIMIT = 48 * 1024 * 1024


def _params(*sem):
    return pltpu.CompilerParams(dimension_semantics=sem, vmem_limit_bytes=VMEM_LIMIT)


def _mm(a, b):
    return jnp.dot(a, b, preferred_element_type=F32)


def _mm_nt(a, b):
    return lax.dot_general(a, b, (((1,), (1,)), ((), ())), preferred_element_type=F32)


def _bmm(a, b):
    return lax.dot_general(a, b, (((2,), (1,)), ((0,), (0,))), preferred_element_type=F32)


def _bmm_nt(a, b):
    return lax.dot_general(a, b, (((2,), (2,)), ((0,), (0,))), preferred_element_type=F32)


def _split(a):
    hi = a.astype(BF16)
    lo = (a - hi.astype(F32)).astype(BF16)
    return hi, lo


def _three_pass(mm, a, b):
    ah, al = _split(a)
    bh, bl = _split(b)
    return mm(ah, bh) + (mm(al, bh) + mm(ah, bl))


def _silu(v):
    return v * jax.nn.sigmoid(v)


def _adaln_kernel(c_ref, w_ref, b_ref, o_ref):
    o_ref[...] = _three_pass(_mm, _silu(c_ref[...]), w_ref[...]) + b_ref[...]


def _adaln(c, w_ada, b_ada):
    b, d = c.shape
    n = w_ada.shape[1] // d
    return pl.pallas_call(
        _adaln_kernel,
        out_shape=jax.ShapeDtypeStruct((b, n * d), F32),
        grid=(n,),
        in_specs=[pl.BlockSpec((b, d), lambda j: (0, 0)),
                  pl.BlockSpec((d, d), lambda j: (0, j)),
                  pl.BlockSpec((1, d), lambda j: (0, j))],
        out_specs=pl.BlockSpec((b, d), lambda j: (0, j)),
        compiler_params=_params("arbitrary"),
        name="adaln",
    )(c, w_ada, b_ada.reshape(1, -1))


PROJ_COLS = 1536
PROJ_TB = 1024
MIX_TB = 512
COL_AQ, COL_DQ, COL_DK, COL_DV, COL_DZ, COL_GA, COL_GB = range(7)
COL_AK = 7 * D_MODEL // ATTN_KV_WIDTH
COL_AV = COL_AK + 1
PROJ_WIDTH = 7 * D_MODEL + 2 * ATTN_KV_WIDTH


def _proj_kernel(x_ref, mod_ref, g_ref, w_ref, wba_ref, o_ref, ba_ref, bat_ref, h_scr):
    @pl.when(pl.program_id(1) == 0)
    def _():
        x = x_ref[...]
        y = x * lax.rsqrt(jnp.mean(x * x, axis=-1, keepdims=True) + EPS) * g_ref[...]
        mod = mod_ref[...]
        h = y * (1.0 + mod[1:2]) + mod[0:1]
        h_scr[...] = h.astype(BF16)
        ba = _three_pass(_mm, h, wba_ref[...])
        ba_ref[...] = ba
        n = ba.shape[1]
        eye = jnp.where(lax.broadcasted_iota(jnp.int32, (n, n), 0) == lax.broadcasted_iota(jnp.int32, (n, n), 1),
                        1.0, 0.0).astype(BF16)
        p0 = ba.astype(BF16)
        r1 = ba - p0.astype(F32)
        p1 = r1.astype(BF16)
        p2 = (r1 - p1.astype(F32)).astype(BF16)
        bat_ref[...] = _mm_nt(eye, p0) + (_mm_nt(eye, p1) + _mm_nt(eye, p2))

    o_ref[...] = _mm(h_scr[...], w_ref[...]).astype(BF16)


def _proj(x2, mod3, gain, w_big, w_ba, seq, tb):
    t, d = x2.shape
    per_seq = seq // tb
    nba = w_ba.shape[1]
    return pl.pallas_call(
        _proj_kernel,
        out_shape=(jax.ShapeDtypeStruct((t, PROJ_WIDTH), BF16),
                   jax.ShapeDtypeStruct((t, nba), F32),
                   jax.ShapeDtypeStruct((nba, t), F32)),
        grid=(t // tb, PROJ_WIDTH // PROJ_COLS),
        in_specs=[pl.BlockSpec((tb, d), lambda i, j: (i, 0)),
                  pl.BlockSpec((None, 6, d), lambda i, j: (i // per_seq, 0, 0)),
                  pl.BlockSpec((1, d), lambda i, j: (0, 0)),
                  pl.BlockSpec((d, PROJ_COLS), lambda i, j: (0, j)),
                  pl.BlockSpec((d, nba), lambda i, j: (0, 0))],
        out_specs=(pl.BlockSpec((tb, PROJ_COLS), lambda i, j: (i, j)),
                   pl.BlockSpec((tb, nba), lambda i, j: (i, 0)),
                   pl.BlockSpec((nba, tb), lambda i, j: (0, i))),
        scratch_shapes=[pltpu.VMEM((tb, d), BF16)],
        compiler_params=_params("arbitrary", "arbitrary"),
        name="proj",
    )(x2, mod3, gain, w_big, w_ba)


ATTN_TQ = 256
LANES = 128


def _head_scale(x, seg, segt):
    hi, lo = _split(x * x)
    ss = _mm(hi, seg) + _mm(lo, seg)
    rh, rl = _split(lax.rsqrt(ss * (1.0 / ATTN_HEAD_DIM) + EPS))
    return _mm(rh, segt) + _mm(rl, segt)


def _attn_kernel(q_ref, kc_ref, vc_ref, kp_ref, vp_ref, qg_ref, kg_ref, sink_ref, seg_ref, segt_ref, o_ref,
                 *, steps_per_seq):
    blk = WINDOW
    tq = q_ref.shape[0]
    first = (pl.program_id(0) % steps_per_seq) == 0
    seg, segt = seg_ref[...], segt_ref[...]
    kvw = ATTN_KV_WIDTH

    q = q_ref[...].astype(F32)
    qn = (q * _head_scale(q, seg, segt) * (qg_ref[...] * (ATTN_HEAD_DIM ** -0.5))).astype(BF16)
    k = jnp.concatenate([kp_ref[...], kc_ref[...]], axis=0).astype(F32)
    kn = k * _head_scale(k, seg[:kvw], segt[:, :kvw]) * kg_ref[...]
    v = jnp.concatenate([vp_ref[...], vc_ref[...]], axis=0).astype(F32)

    left = lax.broadcasted_iota(jnp.int32, (blk + tq, LANES), 1) < ATTN_HEAD_DIM

    def halves(x):
        out = []
        for tile in range(kvw // LANES):
            xt = x[:, tile * LANES:(tile + 1) * LANES]
            xr = pltpu.roll(xt, ATTN_HEAD_DIM, 1)
            out.append((jnp.where(left, xt, 0.0).astype(BF16), jnp.where(left, 0.0, xr).astype(BF16)))
            out.append((jnp.where(left, xr, 0.0).astype(BF16), jnp.where(left, 0.0, xt).astype(BF16)))
        return out

    k_parts, v_parts = halves(kn), halves(v)

    keys = 2 * blk
    qi = lax.broadcasted_iota(jnp.int32, (blk, 2 * keys), 0)
    kj = lax.broadcasted_iota(jnp.int32, (blk, 2 * keys), 1) & (keys - 1)
    rel = qi + blk - kj
    band = (rel >= 0) & (rel < WINDOW)
    col = lax.broadcasted_iota(jnp.int32, (1, 2 * keys), 1)
    sinks = sink_ref[...]
    row = lax.broadcasted_iota(jnp.int32, (2 * keys, LANES), 0)
    lane = lax.broadcasted_iota(jnp.int32, (2 * keys, LANES), 1)
    sink_row = (row & (keys - 1)) == 0
    ones_blk = jnp.where((row < keys) == (lane < ATTN_HEAD_DIM), 1.0, 0.0).astype(BF16)
    pairs = ATTN_HEADS // 2
    per_kv = ATTN_GROUP // 2

    for sb in range(tq // blk):
        krows = slice(sb * blk, sb * blk + keys)
        qrows = slice(sb * blk, (sb + 1) * blk)
        mask = band & (jnp.logical_not(first) | (kj >= blk)) if sb == 0 else band
        kblk = [jnp.concatenate([a[krows], b[krows]], axis=0) for a, b in k_parts]
        vblk = [jnp.where(sink_row, jnp.zeros((), BF16), jnp.concatenate([a[krows], b[krows]], axis=0))
                for a, b in v_parts]
        scores = [_mm_nt(qn[qrows, j * LANES:(j + 1) * LANES], kblk[j // per_kv]) for j in range(pairs)]
        fills = [jnp.where(col == 0, sinks[0:1, 2 * j:2 * j + 1],
                           jnp.where(col == keys, sinks[0:1, 2 * j + 1:2 * j + 2], NEG)) for j in range(pairs)]
        logits = [jnp.where(mask, scores[j], fills[j]) for j in range(pairs)]
        tops = [[jnp.max(s[:, u * keys:(u + 1) * keys], axis=-1, keepdims=True) for u in range(2)] for s in logits]
        probs = [jnp.exp(s - jnp.concatenate([jnp.broadcast_to(m, (blk, keys)) for m in ms], axis=1)).astype(BF16)
                 for s, ms in zip(logits, tops)]
        outs = [_mm(p, vblk[j // per_kv]) for j, p in enumerate(probs)]
        dens = [_mm(p, ones_blk) for p in probs]
        for j in range(pairs):
            o_ref[qrows, j * LANES:(j + 1) * LANES] = (outs[j] / dens[j]).astype(BF16)


def _attention(proj, qn, kn, sinks, seq):
    t = proj.shape[0]
    tq = min(ATTN_TQ, seq)
    per_seq = seq // tq
    halo = tq // WINDOW
    prev = lambda i: jnp.maximum(i * halo - 1, 0)
    head_of_lane = np.arange(ATTN_Q_WIDTH) // ATTN_HEAD_DIM
    seg = (head_of_lane[:, None] == np.arange(ATTN_HEADS)[None, :]).astype(np.float32)
    consts = (jnp.tile(qn, (1, ATTN_HEADS)), jnp.tile(kn, (1, ATTN_KV_HEADS)), sinks,
              jnp.asarray(seg, BF16), jnp.asarray(seg.T, BF16))
    return pl.pallas_call(
        functools.partial(_attn_kernel, steps_per_seq=per_seq),
        out_shape=jax.ShapeDtypeStruct((t, ATTN_Q_WIDTH), BF16),
        grid=(t // tq,),
        in_specs=[pl.BlockSpec((tq, ATTN_Q_WIDTH), lambda i: (i, COL_AQ)),
                  pl.BlockSpec((tq, ATTN_KV_WIDTH), lambda i: (i, COL_AK)),
                  pl.BlockSpec((tq, ATTN_KV_WIDTH), lambda i: (i, COL_AV)),
                  pl.BlockSpec((WINDOW, ATTN_KV_WIDTH), lambda i: (prev(i), COL_AK)),
                  pl.BlockSpec((WINDOW, ATTN_KV_WIDTH), lambda i: (prev(i), COL_AV))]
                 + [pl.BlockSpec(c.shape, lambda i: (0, 0)) for c in consts],
        out_specs=pl.BlockSpec((tq, ATTN_Q_WIDTH), lambda i: (i, 0)),
        compiler_params=_params("arbitrary"),
        name="attn",
    )(proj, proj, proj, proj, proj, *consts)


DN_BLOCK = 128
CONV_HALO = 8


def _dn_kernel(q_ref, k_ref, v_ref, pq_ref, pk_ref, pv_ref, z_ref, ba_ref, bat_ref, cw_ref,
               alog_r, dt_r, alog_c, dt_c, on_ref, o_ref, state, *, blocks_per_seq):
    first = (pl.program_id(0) % blocks_per_seq) == 0
    cb, cl, nh = DN_BLOCK, DN_CHUNK, DN_HEADS

    @pl.when(first)
    def _():
        state[...] = jnp.zeros_like(state)

    def conv(cur_ref, prev_ref, part):
        cur = cur_ref[...].astype(F32)
        prev = jnp.where(first, 0.0, prev_ref[...].astype(F32))
        xc = jnp.concatenate([prev, cur], axis=0)
        acc = None
        for j in range(DN_CONV):
            w = cw_ref[j:j + 1, part * DN_WIDTH:(part + 1) * DN_WIDTH]
            shift = DN_CONV - 1 - j
            xs = pltpu.roll(xc, shift, 0) if shift else xc
            term = w * xs[CONV_HALO:CONV_HALO + cb]
            acc = term if acc is None else acc + term
        return _silu(acc)

    qc, kc, vc = conv(q_ref, pq_ref, 0), conv(k_ref, pk_ref, 1), conv(v_ref, pv_ref, 2)

    ba = ba_ref[...]
    bat = bat_ref[...]
    beta = jax.nn.sigmoid(ba[:, :nh])
    g_col = -jnp.exp(alog_r[...]) * jax.nn.softplus(ba[:, nh:] + dt_r[...])
    g_row = -jnp.exp(alog_c[...]) * jax.nn.softplus(bat[nh:] + dt_c[...])

    r = lax.broadcasted_iota(jnp.int32, (cb, cb), 0)
    c = lax.broadcasted_iota(jnp.int32, (cb, cb), 1)
    same = (r // cl) == (c // cl)
    lower = jnp.where(same & (c <= r), 1.0, 0.0).astype(BF16)
    upper = jnp.where(same & (r <= c), 1.0, 0.0).astype(BF16)

    def exact3(mm, a, b, split_left):
        v0 = a if split_left else b
        p0 = v0.astype(BF16)
        r1 = v0 - p0.astype(F32)
        p1 = r1.astype(BF16)
        p2 = (r1 - p1.astype(F32)).astype(BF16)
        if split_left:
            return mm(p0, b) + (mm(p1, b) + mm(p2, b))
        return mm(a, p0) + (mm(a, p1) + mm(a, p2))

    gc_col = exact3(_mm, lower, g_col, False)
    gc_row = exact3(_mm, g_row, upper, True)

    ri = lax.broadcasted_iota(jnp.int32, (cl, cl), 0)
    ci = lax.broadcasted_iota(jnp.int32, (cl, cl), 1)
    tril = ci <= ri
    strict = ci < ri
    eye = jnp.where(ci == ri, 1.0, 0.0)
    eye_k = jnp.where(lax.broadcasted_iota(jnp.int32, (DN_DIM, DN_DIM), 0)
                      == lax.broadcasted_iota(jnp.int32, (DN_DIM, DN_DIM), 1), 1.0, 0.0).astype(BF16)

    z = z_ref[...].astype(F32)
    for ch in range(cb // cl):
        t0 = ch * cl
        qs, ks, kbs, kgs, vbs, decs, qgs, kds, gls = [], [], [], [], [], [], [], [], []
        for h in range(nh):
            sl = slice(h * DN_DIM, (h + 1) * DN_DIM)
            qh = qc[t0:t0 + cl, sl]
            kh = kc[t0:t0 + cl, sl]
            vh = vc[t0:t0 + cl, sl]
            qh = qh * lax.rsqrt(jnp.sum(qh * qh, axis=-1, keepdims=True) + EPS) * (DN_DIM ** -0.5)
            kh = kh * lax.rsqrt(jnp.sum(kh * kh, axis=-1, keepdims=True) + EPS)
            bh = beta[t0:t0 + cl, h:h + 1]
            gcol = gc_col[t0:t0 + cl, h:h + 1]
            grow = gc_row[h:h + 1, t0:t0 + cl]
            glast = gc_col[t0 + cl - 1:t0 + cl, h:h + 1]
            dec = jnp.where(tril, jnp.exp(jnp.where(tril, gcol - grow, 0.0)), 0.0)
            eg = jnp.exp(gcol)
            kb = kh * bh
            qs.append(qh); ks.append(kh); kbs.append(kb); kgs.append(kb * eg)
            vbs.append(vh * bh); decs.append(dec); qgs.append(qh * eg)
            kds.append(kh * jnp.exp(glast - gcol)); gls.append(jnp.exp(glast))
        q3, k3 = jnp.stack(qs).astype(BF16), jnp.stack(ks).astype(BF16)
        kb3, kg3, vb3 = jnp.stack(kbs).astype(BF16), jnp.stack(kgs).astype(BF16), jnp.stack(vbs).astype(BF16)
        dec3, qg3, kd3 = jnp.stack(decs), jnp.stack(qgs).astype(BF16), jnp.stack(kds).astype(BF16)

        kq = _bmm_nt(jnp.concatenate([kb3, q3], axis=1), k3)
        a_low = jnp.where(strict[None], kq[:, :cl] * dec3, 0.0)
        qk3 = kq[:, cl:] * dec3
        pw = -a_low
        t_mat = eye[None] + pw
        pw = _three_pass(_bmm, pw, pw)
        for m in range(1, 6):
            both = jnp.concatenate([pw, t_mat], axis=1)
            prod = _three_pass(_bmm, both, pw)
            pw = prod[:, :cl]
            t_mat = t_mat + prod[:, cl:]
        uw = _bmm(t_mat.astype(BF16), jnp.concatenate([vb3, kg3], axis=2))
        u3, w3 = uw[:, :, :DN_DIM], uw[:, :, DN_DIM:]

        s_old = state[...]
        s_bf = s_old.astype(BF16)
        ws_qs = _bmm(jnp.concatenate([w3.astype(BF16), qg3], axis=1), s_bf)
        v_new = u3 - ws_qs[:, :cl]
        o3 = ws_qs[:, cl:] + _bmm(qk3.astype(BF16), v_new.astype(BF16))
        kdt = _bmm_nt(jnp.broadcast_to(eye_k[None], (nh, DN_DIM, DN_DIM)), kd3)
        upd = _bmm(kdt.astype(BF16), v_new.astype(BF16))
        state[...] = jnp.stack([s_old[h] * gls[h] for h in range(nh)]) + upd

        for h in range(nh):
            sl = slice(h * DN_DIM, (h + 1) * DN_DIM)
            oh = o3[h]
            oh = oh * lax.rsqrt(jnp.mean(oh * oh, axis=-1, keepdims=True) + EPS) * on_ref[...]
            o_ref[t0:t0 + cl, sl] = (oh * _silu(z[t0:t0 + cl, sl])).astype(BF16)


def _deltanet(proj, ba, bat, conv_w, a_log, dt_bias, out_norm, seq):
    t = proj.shape[0]
    cb = DN_BLOCK
    per_seq = seq // cb
    halo = cb // CONV_HALO
    prev = lambda i: jnp.maximum(i * halo - 1, 0)
    nba = ba.shape[1]
    row = lambda a: a.reshape(1, -1)
    col = lambda a: a.reshape(-1, 1)
    small = lambda shp: pl.BlockSpec(shp, lambda i: (0, 0))
    return pl.pallas_call(
        functools.partial(_dn_kernel, blocks_per_seq=per_seq),
        out_shape=jax.ShapeDtypeStruct((t, DN_WIDTH), BF16),
        grid=(t // cb,),
        in_specs=[pl.BlockSpec((cb, DN_WIDTH), lambda i: (i, COL_DQ)),
                  pl.BlockSpec((cb, DN_WIDTH), lambda i: (i, COL_DK)),
                  pl.BlockSpec((cb, DN_WIDTH), lambda i: (i, COL_DV)),
                  pl.BlockSpec((CONV_HALO, DN_WIDTH), lambda i: (prev(i), COL_DQ)),
                  pl.BlockSpec((CONV_HALO, DN_WIDTH), lambda i: (prev(i), COL_DK)),
                  pl.BlockSpec((CONV_HALO, DN_WIDTH), lambda i: (prev(i), COL_DV)),
                  pl.BlockSpec((cb, DN_WIDTH), lambda i: (i, COL_DZ)),
                  pl.BlockSpec((cb, nba), lambda i: (i, 0)),
                  pl.BlockSpec((nba, cb), lambda i: (0, i)),
                  small((DN_CONV, 3 * DN_WIDTH)),
                  small((1, DN_HEADS)), small((1, DN_HEADS)),
                  small((DN_HEADS, 1)), small((DN_HEADS, 1)),
                  small((1, DN_DIM))],
        out_specs=pl.BlockSpec((cb, DN_WIDTH), lambda i: (i, 0)),
        scratch_shapes=[pltpu.VMEM((DN_HEADS, DN_DIM, DN_DIM), F32)],
        compiler_params=_params("arbitrary"),
        name="deltanet",
    )(proj, proj, proj, proj, proj, proj, proj, ba, bat, conv_w,
      row(a_log), row(dt_bias), col(a_log), col(dt_bias), row(out_norm))


def _mix_kernel(att_ref, od_ref, ga_ref, gb_ref, x_ref, mod_ref, wa_ref, wb_ref, wo_ref, g2_ref,
                wq_ref, x1_ref, h2_ref, q_ref):
    ya = _mm(att_ref[...], wa_ref[...])
    yb = _mm(od_ref[...], wb_ref[...])
    mixed = jax.nn.sigmoid(ga_ref[...].astype(F32)) * ya + jax.nn.sigmoid(gb_ref[...].astype(F32)) * yb
    mod = mod_ref[...]
    x1 = x_ref[...] + mod[2:3] * _mm(mixed.astype(BF16), wo_ref[...])
    x1_ref[...] = x1
    y = x1 * lax.rsqrt(jnp.mean(x1 * x1, axis=-1, keepdims=True) + EPS) * g2_ref[...]
    h2 = y * (1.0 + mod[4:5]) + mod[3:4]
    h2_bf = h2.astype(BF16)
    h2_ref[...] = h2_bf
    q_ref[...] = _mm(h2_bf, wq_ref[...])


def _mix(attn, od, proj, x2, mod3, wa, wb, wo, gain2, wq, seq, tb):
    t, d = x2.shape
    per_seq = seq // tb
    nq = wq.shape[1]
    full = lambda shp: pl.BlockSpec(shp, lambda i: (0, 0))
    return pl.pallas_call(
        _mix_kernel,
        out_shape=(jax.ShapeDtypeStruct((t, d), F32),
                   jax.ShapeDtypeStruct((t, d), BF16),
                   jax.ShapeDtypeStruct((t, nq), F32)),
        grid=(t // tb,),
        in_specs=[pl.BlockSpec((tb, d), lambda i: (i, 0)),
                  pl.BlockSpec((tb, d), lambda i: (i, 0)),
                  pl.BlockSpec((tb, d), lambda i: (i, COL_GA)),
                  pl.BlockSpec((tb, d), lambda i: (i, COL_GB)),
                  pl.BlockSpec((tb, d), lambda i: (i, 0)),
                  pl.BlockSpec((None, 6, d), lambda i: (i // per_seq, 0, 0)),
                  full((d, d)), full((d, d)), full((d, d)), full((1, d)),
                  full((d, nq))],
        out_specs=(pl.BlockSpec((tb, d), lambda i: (i, 0)),
                   pl.BlockSpec((tb, d), lambda i: (i, 0)),
                   pl.BlockSpec((tb, nq), lambda i: (i, 0))),
        compiler_params=_params("arbitrary"),
        name="mix",
    )(attn, od, proj, proj, x2, mod3, wa, wb, wo, gain2, wq)


ROUTE_DTYPE = BF16
ROUTE_TB = 256

_CAND_ROWS = tuple(PEER_TOPK // (a + 1) for a in range(PEER_TOPK))


def _scores(q_ref, keys_ref, hp):
    return _three_pass(_mm_nt, keys_ref[hp], q_ref[:, hp * PEER_HALF:(hp + 1) * PEER_HALF])


_CAND_PAIRS = tuple((a, b) for a in range(PEER_TOPK) for b in range(_CAND_ROWS[a]))
_CAND_PAD = -len(_CAND_PAIRS) % 8
_CAND_N = len(_CAND_PAIRS) + _CAND_PAD


def _select_rows(onehot, vals):
    p0 = vals.astype(BF16)
    r1 = vals - p0.astype(F32)
    p1 = r1.astype(BF16)
    p2 = (r1 - p1.astype(F32)).astype(BF16)
    return _mm(onehot, p0) + (_mm(onehot, p1) + _mm(onehot, p2))


def _merge_network(lo, hi, r):
    step = r * 2
    if step < hi - lo:
        yield from _merge_network(lo, hi, step)
        yield from _merge_network(lo + r, hi, step)
        yield from ((i, i + r) for i in range(lo + r, hi - r, step))
    else:
        yield (lo, lo + r)


def _sort_network(lo, hi):
    if hi > lo:
        mid = lo + (hi - lo) // 2
        yield from _sort_network(lo, mid)
        yield from _sort_network(mid + 1, hi)
        yield from _merge_network(lo, hi, 1)


SUBLANES = 8
_KEY_GROUPS = PEER_N_KEYS // SUBLANES
_SORT_KEY_GROUPS = tuple(_sort_network(0, _KEY_GROUPS - 1))


def _top_values_distinct(scores, ts_scr):
    tb = scores[0].shape[1]
    bad = jnp.zeros((1, tb), F32)
    lists = []
    for s in scores:
        ls = [s[g * SUBLANES:(g + 1) * SUBLANES, :] for g in range(_KEY_GROUPS)]
        for a, b in _SORT_KEY_GROUPS:
            ls[a], ls[b] = jnp.maximum(ls[a], ls[b]), jnp.minimum(ls[a], ls[b])
        lists.append(ls)
    taken = [jnp.zeros((SUBLANES, tb), F32) for _ in scores]
    prev = [None for _ in scores]
    for r in range(PEER_TOPK + 1):
        for p, ls in enumerate(lists):
            m = jnp.max(ls[0], axis=0, keepdims=True)
            win = ls[0] == m
            taken[p] = taken[p] + jnp.where(win, 1.0, 0.0)
            for g in range(PEER_TOPK - r):
                below = ls[g + 1] if g + 1 < _KEY_GROUPS else NEG_INF
                ls[g] = jnp.where(win, below, ls[g])
            if r < PEER_TOPK:
                ts_scr[p, r:r + 1, :] = m
            if prev[p] is not None:
                bad = jnp.maximum(bad, jnp.where(m >= prev[p], 1.0, 0.0))
            prev[p] = m
    for t in taken:
        bad = jnp.maximum(bad, jnp.where(jnp.sum(t, axis=0, keepdims=True) != PEER_TOPK + 1.0, 1.0, 0.0))
    return bad


def _route_pass(q_ref, keys_ref, sel1_ref, sel2_ref, cnt_ref, pad_ref, a_ref, n1_ref, b_ref, r2_ref, ts_scr,
                *, index_order):
    tb = q_ref.shape[0]
    topk = float(PEER_TOPK)
    bad = jnp.zeros((1, tb), F32)
    cand_row = lax.broadcasted_iota(jnp.int32, (_CAND_N, tb), 0)
    key_row = lax.broadcasted_iota(jnp.int32, (PEER_N_KEYS, tb), 0)
    for h in range(PEER_HEADS):
        sc = [_scores(q_ref, keys_ref, 2 * h + p) for p in range(2)]
        rank = [jnp.full(s.shape, topk, F32) for s in sc]
        if index_order:
            work = list(sc)
            for r in range(PEER_TOPK):
                for p in range(2):
                    m = jnp.max(work[p], axis=0, keepdims=True)
                    hit = key_row == jnp.min(jnp.where(work[p] == m, key_row, PEER_N_KEYS), axis=0, keepdims=True)
                    rank[p] = jnp.where(hit, float(r), rank[p])
                    work[p] = jnp.where(hit, NEG_INF, work[p])
                    ts_scr[p, r:r + 1, :] = m
        else:
            bad = jnp.maximum(bad, _top_values_distinct(sc, ts_scr))
        ts1, ts2 = ts_scr[0], ts_scr[1]
        if not index_order:
            rank[1] = functools.reduce(lambda acc, r: acc + jnp.where(ts2[r:r + 1] > sc[1], 1.0, 0.0),
                                       range(PEER_TOPK), jnp.zeros(sc[1].shape, F32))
        cand = (_select_rows(sel1_ref[...], ts1) + _select_rows(sel2_ref[...], ts2)) + pad_ref[...]
        m0 = ts1[0:1] + ts2[0:1]
        zsum = jnp.zeros_like(m0)
        for r in range(PEER_TOPK):
            m = jnp.max(cand, axis=0, keepdims=True)
            first = jnp.min(jnp.where(cand == m, cand_row, _CAND_N), axis=0, keepdims=True)
            cand = jnp.where(cand_row == first, NEG_INF, cand)
            zsum = zsum + jnp.exp(m - m0)
        gone = jnp.where(cand == NEG_INF, 1.0, 0.0)
        per_rank = _mm(cnt_ref[...], gone.astype(BF16))
        n1 = jnp.zeros(sc[0].shape, F32)
        for a in range(PEER_TOPK):
            is_rank_a = rank[0] == float(a) if index_order else sc[0] == ts1[a:a + 1]
            n1 = jnp.where(is_rank_a, per_rank[a:a + 1], n1)
        in1 = rank[0] < topk if index_order else sc[0] >= ts1[PEER_TOPK - 1:PEER_TOPK]
        in2 = rank[1] < topk
        a_ref[h] = jnp.where(in1, jnp.exp(jnp.where(in1, sc[0] - ts1[0:1], 0.0)), 0.0)
        n1_ref[h] = n1
        b_ref[h] = jnp.where(in2, jnp.exp(jnp.where(in2, sc[1] - ts2[0:1], 0.0)) / zsum, 0.0).astype(ROUTE_DTYPE)
        r2_ref[h] = rank[1].astype(ROUTE_DTYPE)
    return bad


def _route_kernel(*refs):
    bad = _route_pass(*refs, index_order=False)

    @pl.when(jnp.max(bad) > 0.0)
    def _():
        _route_pass(*refs, index_order=True)


def _route(q, keys, tb):
    t = q.shape[0]
    sel1 = np.zeros((_CAND_N, PEER_TOPK), np.float32)
    sel2 = np.zeros((_CAND_N, PEER_TOPK), np.float32)
    pad = np.full((_CAND_N, 1), NEG_INF, np.float32)
    for row, (a, b) in enumerate(_CAND_PAIRS):
        sel1[row, a] = 1.0
        sel2[row, b] = 1.0
        pad[row, 0] = 0.0
    consts = (jnp.asarray(sel1, BF16), jnp.asarray(sel2, BF16), jnp.asarray(sel1.T, BF16), jnp.asarray(pad))
    row_shp = jax.ShapeDtypeStruct((PEER_HEADS, PEER_N_KEYS, t), F32)
    key_shp = jax.ShapeDtypeStruct((PEER_HEADS, PEER_N_KEYS, t), ROUTE_DTYPE)
    ospec = pl.BlockSpec((PEER_HEADS, PEER_N_KEYS, tb), lambda i: (0, 0, i))
    return pl.pallas_call(
        _route_kernel,
        out_shape=(row_shp, row_shp, key_shp, key_shp),
        grid=(t // tb,),
        in_specs=[pl.BlockSpec((tb, q.shape[1]), lambda i: (i, 0)),
                  pl.BlockSpec(keys.shape, lambda i: (0, 0, 0))]
                 + [pl.BlockSpec(c.shape, lambda i: (0, 0)) for c in consts],
        out_specs=(ospec, ospec, ospec, ospec),
        scratch_shapes=[pltpu.VMEM((2, PEER_TOPK, tb), F32)],
        compiler_params=_params("arbitrary"),
        name="route",
    )(q, keys, *consts)


EXPERT_ROWS = 16
EXPERT_TB = 512
BF16_ROWS = 16
EXPERT_SLICES = 4


def _experts_kernel(h2_ref, u_ref, vt_ref, a_ref, n1_ref, b_ref, r2_ref, x1_ref, mod_ref, o_ref, acc, act):
    e = pl.program_id(1)

    @pl.when(e == 0)
    def _():
        acc[...] = jnp.zeros_like(acc)

    nk = PEER_N_KEYS
    tb = acc.shape[1]
    per = EXPERT_ROWS // EXPERT_SLICES
    rows_of = lambda q: slice(q * per * nk, (q + 1) * per * nk)
    hid = [None] * EXPERT_SLICES
    hid[0] = _mm_nt(u_ref[rows_of(0), :], h2_ref[...])
    for q in range(EXPERT_SLICES):
        if q + 1 < EXPERT_SLICES:
            hid[q + 1] = _mm_nt(u_ref[rows_of(q + 1), :], h2_ref[...])
        for k in range(per):
            ii = q * per + k
            gate = None
            for h in range(PEER_HEADS):
                count = jnp.broadcast_to(n1_ref[h, ii:ii + 1, :], (BF16_ROWS, tb)).astype(BF16)[None]
                scale = jnp.broadcast_to(0.5 * a_ref[h, ii:ii + 1, :], (BF16_ROWS, tb)).astype(BF16)[None]
                term = jnp.where(r2_ref[h] < count, b_ref[h], jnp.zeros((), BF16)) * scale
                gate = term if gate is None else gate + term
            v = hid[q][k * nk:(k + 1) * nk, :]
            act[ii * nk:(ii + 1) * nk, :] = ((v * (1.0 + lax.erf(v * (2.0 ** -0.5)))).astype(BF16)
                                             * gate.reshape(nk, tb))
    acc[...] += _mm(vt_ref[...], act[...])

    @pl.when(e == pl.num_programs(1) - 1)
    def _():
        o_ref[...] = x1_ref[...] + mod_ref[5:6, :] * acc[...].T


def _experts(h2, u_bf, vt_bf, a, n1, b, r2, x1, mod3, seq, tb):
    t, d = h2.shape
    per_seq = seq // tb
    ne = u_bf.shape[0]
    eb = EXPERT_ROWS * PEER_N_KEYS
    tiles = PEER_N_KEYS // BF16_ROWS
    b = b.reshape(PEER_HEADS, tiles, BF16_ROWS, t)
    r2 = r2.reshape(PEER_HEADS, tiles, BF16_ROWS, t)
    rows = pl.BlockSpec((PEER_HEADS, EXPERT_ROWS, tb), lambda i, e: (0, e, i))
    allk = pl.BlockSpec((PEER_HEADS, tiles, BF16_ROWS, tb), lambda i, e: (0, 0, 0, i))
    return pl.pallas_call(
        _experts_kernel,
        out_shape=jax.ShapeDtypeStruct((t, d), F32),
        grid=(t // tb, ne // eb),
        in_specs=[pl.BlockSpec((tb, d), lambda i, e: (i, 0)),
                  pl.BlockSpec((eb, d), lambda i, e: (e, 0)),
                  pl.BlockSpec((d, eb), lambda i, e: (0, e)),
                  rows, rows, allk, allk,
                  pl.BlockSpec((tb, d), lambda i, e: (i, 0)),
                  pl.BlockSpec((None, 6, d), lambda i, e: (i // per_seq, 0, 0))],
        out_specs=pl.BlockSpec((tb, d), lambda i, e: (i, 0)),
        scratch_shapes=[pltpu.VMEM((d, tb), F32), pltpu.VMEM((eb, tb), BF16)],
        compiler_params=_params("arbitrary", "arbitrary"),
        name="experts",
    )(h2, u_bf, vt_bf, a, n1, b, r2, x1, mod3)


def _regroup_w_in(w):
    d = D_MODEL
    o = 0
    parts = {}
    for name, width in (("aq", ATTN_Q_WIDTH), ("ak", ATTN_KV_WIDTH), ("av", ATTN_KV_WIDTH), ("dq", DN_WIDTH),
                        ("dk", DN_WIDTH), ("dv", DN_WIDTH), ("dz", DN_WIDTH), ("ba", 2 * DN_HEADS),
                        ("ga", d), ("gb", d)):
        parts[name] = w[:, o:o + width]
        o += width
    big = jnp.concatenate([parts[n] for n in ("aq", "dq", "dk", "dv", "dz", "ga", "gb", "ak", "av")], axis=1)
    return big.astype(BF16), parts["ba"]


def _layer(x2, c, seq, w_ada, b_ada, norm1_gain, w_in, attn_q_norm, attn_k_norm, attn_sinks, dn_conv_w, dn_a_log,
           dn_dt_bias, dn_out_norm, w_attn_branch, w_dn_branch, w_out, norm2_gain, peer_w_q, peer_sub_keys,
           peer_u, peer_v):
    d = D_MODEL
    row = lambda a: a.reshape(1, -1)
    tb = min(MIX_TB, seq)

    mod3 = _adaln(c, w_ada, b_ada).reshape(c.shape[0], 6, d)
    w_big, w_ba = _regroup_w_in(w_in)
    proj, ba, bat = _proj(x2, mod3, row(norm1_gain), w_big, w_ba, seq, min(PROJ_TB, seq))
    attn = _attention(proj, row(attn_q_norm), row(attn_k_norm), row(attn_sinks), seq)
    od = _deltanet(proj, ba, bat, dn_conv_w, dn_a_log, dn_dt_bias, dn_out_norm, seq)
    x1, h2, q = _mix(attn, od, proj, x2, mod3, w_attn_branch.astype(BF16), w_dn_branch.astype(BF16),
                     w_out.astype(BF16), row(norm2_gain), peer_w_q.astype(BF16), seq, tb)
    keys = peer_sub_keys.reshape(2 * PEER_HEADS, PEER_N_KEYS, PEER_HALF)
    a, n1, b, r2 = _route(q, keys, min(ROUTE_TB, seq))
    return _experts(h2, peer_u.astype(BF16), peer_v.T.astype(BF16), a, n1, b, r2, x1, mod3, seq, min(EXPERT_TB, seq))


def kernel(x, c, w_ada, b_ada, norm1_gain, w_in, attn_q_norm, attn_k_norm, attn_sinks, dn_conv_w, dn_a_log, dn_dt_bias, dn_out_norm, w_attn_branch, w_dn_branch, w_out, norm2_gain, peer_w_q, peer_sub_keys, peer_u, peer_v):
    b, s, d = x.shape
    x2 = x.reshape(b * s, d)
    for l in range(w_ada.shape[0]):
        x2 = _layer(x2, c, s, w_ada[l], b_ada[l], norm1_gain[l], w_in[l], attn_q_norm[l], attn_k_norm[l],
                    attn_sinks[l], dn_conv_w[l], dn_a_log[l], dn_dt_bias[l], dn_out_norm[l], w_attn_branch[l],
                    w_dn_branch[l], w_out[l], norm2_gain[l], peer_w_q[l], peer_sub_keys[l], peer_u[l], peer_v[l])
    return x2.reshape(b, s, d)
```
